```python
import jax, jax.numpy as jnp
from jax import lax
import numpy as np

D_MODEL = 1024
BATCH = 8
SEQ = 4096
DEPTH = 2

CHUNK = 64
Q_BLOCK = 128
HEAD_DIM = 64
CONV_WIDTH = D_MODEL // 2
LRU_WIDTH = D_MODEL // 2
CONV_GROUPS = CONV_WIDTH // HEAD_DIM
LRU_HEADS = LRU_WIDTH // HEAD_DIM
SHORT_CONV_K = 3
LRU_CONV_K = 4
RG_C = 8.0
SB_HEADS = D_MODEL // HEAD_DIM
D_FF = ((8 * D_MODEL // 3 + 255) // 256) * 256
N_MOD = 6
EPS = 1e-6

kernel_name = 'hybrid_shortconv_rglru_stickbreaking_adaln'


def rms_norm(x, g):
    x32 = x.astype(jnp.float32)
    y = x32 * lax.rsqrt(jnp.mean(x32 * x32, axis=-1, keepdims=True) + EPS)
    return (y * g.astype(jnp.float32)).astype(x.dtype)


def ada_modulation(c, ada_w, ada_b):
    m = jax.nn.silu(c) @ ada_w + ada_b
    return jnp.split(m, N_MOD, axis=-1)


def modulate(h, shift, scale):
    return h * (1.0 + scale[:, None, :]) + shift[:, None, :]


def causal_depthwise_conv(x, w):
    k_width = w.shape[0]
    seq = x.shape[1]
    xp = jnp.pad(x, ((0, 0), (k_width - 1, 0), (0, 0)))
    y = xp[:, 0:seq] * w[0]
    for k in range(1, k_width):
        y = y + xp[:, k:k + seq] * w[k]
    return y


def rg_lru(xr, w_a, b_a, w_x, b_x, lam):
    bsz, seq, width = xr.shape
    n_blk, blk = w_a.shape[0], w_a.shape[1]
    x32 = xr.astype(jnp.float32)
    xh = x32.reshape(bsz, seq, n_blk, blk)
    r = jax.nn.sigmoid(jnp.einsum('bshi,hij->bshj', xh, w_a.astype(jnp.float32)) + b_a).reshape(bsz, seq, width)
    i = jax.nn.sigmoid(jnp.einsum('bshi,hij->bshj', xh, w_x.astype(jnp.float32)) + b_x).reshape(bsz, seq, width)
    log_a = -RG_C * r * jax.nn.softplus(-lam.astype(jnp.float32))
    a = jnp.exp(log_a)
    b = jnp.sqrt(-jnp.expm1(2.0 * log_a)) * (i * x32)
    n_chunks = seq // CHUNK
    a_c = a.reshape(bsz, n_chunks, CHUNK, width).transpose(1, 0, 2, 3)
    b_c = b.reshape(bsz, n_chunks, CHUNK, width).transpose(1, 0, 2, 3)

    def combine(lhs, rhs):
        al, bl = lhs
        ar, br = rhs
        return al * ar, ar * bl + br

    def chunk_step(h0, ab):
        a_k, b_k = ab
        a_cum, b_cum = lax.associative_scan(combine, (a_k, b_k), axis=1)
        h = a_cum * h0[:, None, :] + b_cum
        return h[:, -1], h

    h_init = jnp.zeros((bsz, width), jnp.float32)
    _, hs = lax.scan(chunk_step, h_init, (a_c, b_c))
    return hs.transpose(1, 0, 2, 3).reshape(bsz, seq, width)


def stick_breaking_attention(q, k, v):
    seq, dh = q.shape[2], q.shape[3]
    scale = dh ** -0.5
    outs = []
    for blk in range(seq // Q_BLOCK):
        q0, q1 = blk * Q_BLOCK, (blk + 1) * Q_BLOCK
        qb = q[:, :, q0:q1].astype(jnp.float32)
        kb = k[:, :, :q1].astype(jnp.float32)
        vb = v[:, :, :q1].astype(jnp.float32)
        z = jnp.einsum('bhqd,bhkd->bhqk', qb, kb) * scale
        q_pos = jnp.arange(q0, q1)[:, None]
        k_pos = jnp.arange(q1)[None, :]
        strict = k_pos < q_pos
        log_keep = jnp.where(strict, jax.nn.log_sigmoid(-z), 0.0)
        prefix = jnp.cumsum(log_keep, axis=-1)
        after = prefix[..., -1:] - prefix
        w = jnp.where(strict, jnp.exp(jax.nn.log_sigmoid(z) + after), 0.0)
        outs.append(jnp.einsum('bhqk,bhkd->bhqd', w, vb))
    return jnp.concatenate(outs, axis=2)


def swiglu(h, w_gate, w_up, w_down):
    return (jax.nn.silu(h @ w_gate) * (h @ w_up)) @ w_down


def even_layer(x, c, ada_w, ada_b, mix_norm, w_in, conv_a_w, conv_b_w, conv_b_b,
               rg_a_w, rg_a_b, rg_x_w, rg_x_b, rg_lambda, w_out,
               ffn_norm, ffn_w_gate, ffn_w_up, ffn_w_down):
    sh_m, sc_m, g_m, sh_f, sc_f, g_f = ada_modulation(c, ada_w, ada_b)
    h = modulate(rms_norm(x, mix_norm), sh_m, sc_m)
    u = h @ w_in
    cuts = [CONV_WIDTH, 2 * CONV_WIDTH, 3 * CONV_WIDTH, 3 * CONV_WIDTH + LRU_WIDTH]
    a_b, a_c, a_x, r_gate, r_x = jnp.split(u, cuts, axis=-1)
    y_a = a_b * causal_depthwise_conv(a_c * a_x, conv_a_w)
    xr = causal_depthwise_conv(r_x, conv_b_w) + conv_b_b
    y_b = jax.nn.gelu(r_gate) * rg_lru(xr, rg_a_w, rg_a_b, rg_x_w, rg_x_b, rg_lambda).astype(x.dtype)
    y = jnp.concatenate([y_a, y_b], axis=-1) @ w_out
    x = x + g_m[:, None, :] * y
    hf = modulate(rms_norm(x, ffn_norm), sh_f, sc_f)
    return x + g_f[:, None, :] * swiglu(hf, ffn_w_gate, ffn_w_up, ffn_w_down)


def odd_layer(x, c, ada_w, ada_b, mix_norm, w_qkv, q_norm, k_norm, w_out,
              ffn_norm, ffn_w_gate, ffn_w_up, ffn_w_down):
    sh_m, sc_m, g_m, sh_f, sc_f, g_f = ada_modulation(c, ada_w, ada_b)
    h = modulate(rms_norm(x, mix_norm), sh_m, sc_m)
    bsz, seq, _ = x.shape
    qkv = (h @ w_qkv).reshape(bsz, seq, 3, SB_HEADS, HEAD_DIM)
    q = rms_norm(qkv[:, :, 0], q_norm).transpose(0, 2, 1, 3)
    k = rms_norm(qkv[:, :, 1], k_norm).transpose(0, 2, 1, 3)
    v = qkv[:, :, 2].transpose(0, 2, 1, 3)
    o = stick_breaking_attention(q, k, v).astype(x.dtype)
    o = o.transpose(0, 2, 1, 3).reshape(bsz, seq, SB_HEADS * HEAD_DIM)
    x = x + g_m[:, None, :] * (o @ w_out)
    hf = modulate(rms_norm(x, ffn_norm), sh_f, sc_f)
    return x + g_f[:, None, :] * swiglu(hf, ffn_w_gate, ffn_w_up, ffn_w_down)


def setup_inputs(seed: int = 0) -> dict:
    key = jax.random.key(seed)
    ks = jax.random.split(key, 40)
    f32 = jnp.float32

    def nrm(k, shape, scale):
        return jax.random.normal(k, shape, f32) * scale

    d = D_MODEL
    u = jax.random.uniform(ks[12], (LRU_WIDTH,), f32, 0.9, 0.999)
    a0 = u ** (1.0 / RG_C)
    lam = jnp.log(a0) - jnp.log1p(-a0)
    return {
        'x': nrm(ks[0], (BATCH, SEQ, d), 1.0),
        'c': nrm(ks[1], (BATCH, d), 1.0),
        'l0_ada_w': nrm(ks[2], (d, N_MOD * d), 0.5 * d ** -0.5),
        'l0_ada_b': nrm(ks[3], (N_MOD * d,), 0.02),
        'l0_mix_norm': 1.0 + nrm(ks[4], (d,), 0.02),
        'l0_w_in': nrm(ks[5], (d, 3 * CONV_WIDTH + 2 * LRU_WIDTH), d ** -0.5),
        'l0_conv_a_w': nrm(ks[6], (SHORT_CONV_K, CONV_WIDTH), SHORT_CONV_K ** -0.5),
        'l0_conv_b_w': nrm(ks[7], (LRU_CONV_K, LRU_WIDTH), LRU_CONV_K ** -0.5),
        'l0_conv_b_b': nrm(ks[8], (LRU_WIDTH,), 0.01),
        'l0_rg_a_w': nrm(ks[9], (LRU_HEADS, HEAD_DIM, HEAD_DIM), HEAD_DIM ** -0.5),
        'l0_rg_a_b': nrm(ks[10], (LRU_HEADS, HEAD_DIM), 0.01),
        'l0_rg_x_w': nrm(ks[11], (LRU_HEADS, HEAD_DIM, HEAD_DIM), HEAD_DIM ** -0.5),
        'l0_rg_x_b': nrm(ks[13], (LRU_HEADS, HEAD_DIM), 0.01),
        'l0_rg_lambda': lam,
        'l0_w_out': nrm(ks[14], (CONV_WIDTH + LRU_WIDTH, d), (CONV_WIDTH + LRU_WIDTH) ** -0.5),
        'l0_ffn_norm': 1.0 + nrm(ks[15], (d,), 0.02),
        'l0_ffn_w_gate': nrm(ks[16], (d, D_FF), d ** -0.5),
        'l0_ffn_w_up': nrm(ks[17], (d, D_FF), d ** -0.5),
        'l0_ffn_w_down': nrm(ks[18], (D_FF, d), D_FF ** -0.5),
        'l1_ada_w': nrm(ks[19], (d, N_MOD * d), 0.5 * d ** -0.5),
        'l1_ada_b': nrm(ks[20], (N_MOD * d,), 0.02),
        'l1_mix_norm': 1.0 + nrm(ks[21], (d,), 0.02),
        'l1_w_qkv': nrm(ks[22], (d, 3 * SB_HEADS * HEAD_DIM), d ** -0.5),
        'l1_q_norm': 1.0 + nrm(ks[23], (HEAD_DIM,), 0.02),
        'l1_k_norm': 1.0 + nrm(ks[24], (HEAD_DIM,), 0.02),
        'l1_w_out': nrm(ks[25], (SB_HEADS * HEAD_DIM, d), (SB_HEADS * HEAD_DIM) ** -0.5),
        'l1_ffn_norm': 1.0 + nrm(ks[26], (d,), 0.02),
        'l1_ffn_w_gate': nrm(ks[27], (d, D_FF), d ** -0.5),
        'l1_ffn_w_up': nrm(ks[28], (d, D_FF), d ** -0.5),
        'l1_ffn_w_down': nrm(ks[29], (D_FF, d), D_FF ** -0.5),
    }


def reference(x, c,
              l0_ada_w, l0_ada_b, l0_mix_norm, l0_w_in, l0_conv_a_w, l0_conv_b_w, l0_conv_b_b,
              l0_rg_a_w, l0_rg_a_b, l0_rg_x_w, l0_rg_x_b, l0_rg_lambda, l0_w_out,
              l0_ffn_norm, l0_ffn_w_gate, l0_ffn_w_up, l0_ffn_w_down,
              l1_ada_w, l1_ada_b, l1_mix_norm, l1_w_qkv, l1_q_norm, l1_k_norm, l1_w_out,
              l1_ffn_norm, l1_ffn_w_gate, l1_ffn_w_up, l1_ffn_w_down):
    even_params = [(l0_ada_w, l0_ada_b, l0_mix_norm, l0_w_in, l0_conv_a_w, l0_conv_b_w, l0_conv_b_b,
                    l0_rg_a_w, l0_rg_a_b, l0_rg_x_w, l0_rg_x_b, l0_rg_lambda, l0_w_out,
                    l0_ffn_norm, l0_ffn_w_gate, l0_ffn_w_up, l0_ffn_w_down)]
    odd_params = [(l1_ada_w, l1_ada_b, l1_mix_norm, l1_w_qkv, l1_q_norm, l1_k_norm, l1_w_out,
                   l1_ffn_norm, l1_ffn_w_gate, l1_ffn_w_up, l1_ffn_w_down)]
    for layer in range(DEPTH):
        if layer % 2 == 0:
            x = even_layer(x, c, *even_params[layer // 2])
        else:
            x = odd_layer(x, c, *odd_params[layer // 2])
    return x
```

```python
import functools
import math

import jax
import jax.numpy as jnp
from jax import lax
from jax.experimental import pallas as pl
from jax.experimental.pallas import tpu as pltpu

D_MODEL = 1024
HEAD_DIM = 64
CONV_WIDTH = D_MODEL // 2
LRU_WIDTH = D_MODEL // 2
SHORT_CONV_K = 3
LRU_CONV_K = 4
RG_C = 8.0
N_MOD = 6
EPS = 1e-6

V7X_LANES = 128
V7X_SUBLANES = 8
V7X_MXU_DIM = 256
V7X_VMEM_LIMIT_BYTES = 56 * 1024 * 1024

HEADS_PER_LANE_GROUP = V7X_LANES // HEAD_DIM

MIX_ROWS = 256
FFN_ROWS = 512
QKV_ROWS = 512
ATT_BLOCK = 256
FFN_CHUNK = V7X_MXU_DIM
ADA_COLS = 1536


def _const_spec(shape):
    zeros = (0,) * len(shape)
    return pl.BlockSpec(shape, lambda *_: zeros, pipeline_mode=pl.Buffered(1))


def _sigmoid(x):
    return 1.0 / (1.0 + jnp.exp(-x))


def _softplus(x):
    return jnp.maximum(x, 0.0) + jnp.log(1.0 + jnp.exp(-jnp.abs(x)))


def _gelu_tanh(x):
    c = math.sqrt(2.0 / math.pi)
    return 0.5 * x * (1.0 + jnp.tanh(c * (x + 0.044715 * (x * x * x))))


def _rms_norm_modulate(x, g, shift, scale):
    ms = jnp.mean(x * x, axis=-1, keepdims=True)
    y = x * lax.rsqrt(ms + EPS) * g
    return y * (1.0 + scale) + shift


def _split_bf16(x):
    hi = x.astype(jnp.bfloat16)
    lo = (x - hi.astype(jnp.float32)).astype(jnp.bfloat16)
    return hi, lo


def _dot(a, b):
    return jnp.dot(a, b, preferred_element_type=jnp.float32)


def _ada_kernel(c_ref, w_ref, b_ref, o_ref):
    c = c_ref[...]
    s = c * _sigmoid(c)
    s_hi, s_lo = _split_bf16(s)
    w_hi, w_lo = _split_bf16(w_ref[...])
    acc = _dot(s_hi, w_hi) + _dot(s_hi, w_lo) + _dot(s_lo, w_hi)
    o_ref[...] = acc + b_ref[...]


def _ada_modulation(c, ada_w, ada_b):
    bsz, d = c.shape
    n = ada_w.shape[1]
    assert n % ADA_COLS == 0
    out = pl.pallas_call(
        _ada_kernel,
        grid=(n // ADA_COLS,),
        in_specs=[
            pl.BlockSpec((bsz, d), lambda j: (0, 0)),
            pl.BlockSpec((d, ADA_COLS), lambda j: (0, j)),
            pl.BlockSpec((1, ADA_COLS), lambda j: (0, j)),
        ],
        out_specs=pl.BlockSpec((bsz, ADA_COLS), lambda j: (0, j)),
        out_shape=jax.ShapeDtypeStruct((bsz, n), jnp.float32),
        compiler_params=pltpu.CompilerParams(dimension_semantics=("arbitrary",)),
        name="ada_modulation",
    )(c, ada_w, ada_b.reshape(1, n))
    return out.reshape(bsz, N_MOD, d)


def _shift_rows(x, d, fill, row):
    return jnp.where(row >= d, pltpu.roll(x, d, 0), fill)


def _linear_scan(a, b):
    n = a.shape[0]
    row = lax.broadcasted_iota(jnp.int32, a.shape, 0)
    d = 1
    while d < n:
        a_prev = _shift_rows(a, d, 1.0, row)
        b_prev = _shift_rows(b, d, 0.0, row)
        b = a * b_prev + b
        a = a * a_prev
        d *= 2
    return a, b


def _mixer_kernel(x_ref, mod_ref, norm_ref, w_in_ref, conv_a_ref, conv_b_ref, conv_bb_ref,
                  wa_ref, ba_ref, wx_ref, bx_ref, lam_ref, y_ref,
                  p_hist, r_hist, h_state):
    rows = x_ref.shape[1]
    hist = V7X_SUBLANES
    first = pl.program_id(1) == 0

    @pl.when(first)
    def _():
        p_hist[0:hist, :] = jnp.zeros((hist, CONV_WIDTH), jnp.float32)
        r_hist[0:hist, :] = jnp.zeros((hist, LRU_WIDTH), jnp.float32)
        h_state[...] = jnp.zeros_like(h_state)

    @pl.when(jnp.logical_not(first))
    def _():
        p_hist[0:hist, :] = p_hist[rows:rows + hist, :]
        r_hist[0:hist, :] = r_hist[rows:rows + hist, :]

    x = x_ref[0]
    h = _rms_norm_modulate(x, norm_ref[...], mod_ref[0, 0:1, :], mod_ref[0, 1:2, :])
    u = _dot(h.astype(jnp.bfloat16), w_in_ref[...])
    w = CONV_WIDTH
    a_b, a_c, a_x = u[:, 0:w], u[:, w:2 * w], u[:, 2 * w:3 * w]
    r_gate, r_x = u[:, 3 * w:4 * w], u[:, 4 * w:5 * w]

    p = a_c * a_x
    p_hist[hist:hist + rows, :] = p
    conv = conv_a_ref[SHORT_CONV_K - 1:SHORT_CONV_K, :] * p
    for k in range(SHORT_CONV_K - 1):
        back = SHORT_CONV_K - 1 - k
        conv = conv + conv_a_ref[k:k + 1, :] * p_hist[pl.ds(hist - back, rows), :]
    y_ref[0, :, 0:w] = (a_b * conv).astype(y_ref.dtype)

    r_hist[hist:hist + rows, :] = r_x
    xr = conv_b_ref[LRU_CONV_K - 1:LRU_CONV_K, :] * r_x + conv_bb_ref[...]
    for k in range(LRU_CONV_K - 1):
        back = LRU_CONV_K - 1 - k
        xr = xr + conv_b_ref[k:k + 1, :] * r_hist[pl.ds(hist - back, rows), :]
    xr16 = xr.astype(jnp.bfloat16)
    r = _sigmoid(_dot(xr16, wa_ref[...]) + ba_ref[...])
    i = _sigmoid(_dot(xr16, wx_ref[...]) + bx_ref[...])
    log_a = (-RG_C) * r * _softplus(-lam_ref[...])
    a = jnp.exp(log_a)
    b = jnp.sqrt(1.0 - a * a) * (i * xr)
    a_cum, b_cum = _linear_scan(a, b)
    hs = a_cum * h_state[0:1, :] + b_cum
    h_state[0:1, :] = hs[rows - 1:rows, :]
    y_ref[0, :, w:2 * w] = (_gelu_tanh(r_gate) * hs).astype(y_ref.dtype)


def _block_diag(w):
    n_blk, blk, _ = w.shape
    eye = jnp.eye(n_blk, dtype=w.dtype)
    return jnp.einsum('hij,hg->higj', w, eye).reshape(n_blk * blk, n_blk * blk)


def _mixer(x, mod, mix_norm, w_in, conv_a_w, conv_b_w, conv_b_b,
           rg_a_w, rg_a_b, rg_x_w, rg_x_b, rg_lambda):
    bsz, seq, d = x.shape
    rows = MIX_ROWS
    assert seq % rows == 0
    n_in = w_in.shape[1]
    w = LRU_WIDTH
    return pl.pallas_call(
        _mixer_kernel,
        grid=(bsz, seq // rows),
        in_specs=[
            pl.BlockSpec((1, rows, d), lambda b, s: (b, s, 0)),
            pl.BlockSpec((1, N_MOD, d), lambda b, s: (b, 0, 0)),
            _const_spec((1, d)),
            _const_spec((d, n_in)),
            _const_spec((SHORT_CONV_K, CONV_WIDTH)),
            _const_spec((LRU_CONV_K, w)),
            _const_spec((1, w)),
            _const_spec((w, w)),
            _const_spec((1, w)),
            _const_spec((w, w)),
            _const_spec((1, w)),
            _const_spec((1, w)),
        ],
        out_specs=pl.BlockSpec((1, rows, d), lambda b, s: (b, s, 0)),
        out_shape=jax.ShapeDtypeStruct((bsz, seq, d), jnp.bfloat16),
        scratch_shapes=[
            pltpu.VMEM((V7X_SUBLANES + rows, CONV_WIDTH), jnp.float32),
            pltpu.VMEM((V7X_SUBLANES + rows, w), jnp.float32),
            pltpu.VMEM((V7X_SUBLANES, w), jnp.float32),
        ],
        compiler_params=pltpu.CompilerParams(
            dimension_semantics=("arbitrary", "arbitrary"),
            vmem_limit_bytes=V7X_VMEM_LIMIT_BYTES),
        name="l0_mixer",
    )(x, mod, mix_norm.reshape(1, d), w_in.astype(jnp.bfloat16), conv_a_w, conv_b_w,
      conv_b_b.reshape(1, w),
      _block_diag(rg_a_w).astype(jnp.bfloat16), rg_a_b.reshape(1, w),
      _block_diag(rg_x_w).astype(jnp.bfloat16), rg_x_b.reshape(1, w),
      rg_lambda.reshape(1, w))


def _proj_ffn_kernel(x_ref, y_ref, mod_ref, w_out_ref, norm_ref, wg_ref, wu_ref, wd_ref,
                     o_ref, acc_ref):
    x = x_ref[0]
    x1 = x + mod_ref[0, 2:3, :] * _dot(y_ref[0], w_out_ref[...])
    hf = _rms_norm_modulate(x1, norm_ref[...], mod_ref[0, 3:4, :], mod_ref[0, 4:5, :])
    hf16 = hf.astype(jnp.bfloat16)
    n_chunks = wg_ref.shape[0]
    for c in range(n_chunks):
        g = _dot(hf16, wg_ref[c])
        up = _dot(hf16, wu_ref[c])
        act = (g * _sigmoid(g) * up).astype(jnp.bfloat16)
        part = _dot(act, wd_ref[c])
        if c == 0:
            acc_ref[...] = part
        else:
            acc_ref[...] += part
    o_ref[0] = x1 + mod_ref[0, 5:6, :] * acc_ref[...]


def _proj_ffn(x, y, mod, w_out, ffn_norm, w_gate, w_up, w_down):
    bsz, seq, d = x.shape
    rows = FFN_ROWS
    d_ff = w_gate.shape[1]
    assert seq % rows == 0 and d_ff % FFN_CHUNK == 0
    n_chunks = d_ff // FFN_CHUNK
    wg = w_gate.astype(jnp.bfloat16).reshape(d, n_chunks, FFN_CHUNK).transpose(1, 0, 2)
    wu = w_up.astype(jnp.bfloat16).reshape(d, n_chunks, FFN_CHUNK).transpose(1, 0, 2)
    wd = w_down.astype(jnp.bfloat16).reshape(n_chunks, FFN_CHUNK, d)
    return pl.pallas_call(
        _proj_ffn_kernel,
        grid=(bsz, seq // rows),
        in_specs=[
            pl.BlockSpec((1, rows, d), lambda b, s: (b, s, 0)),
            pl.BlockSpec((1, rows, d), lambda b, s: (b, s, 0)),
            pl.BlockSpec((1, N_MOD, d), lambda b, s: (b, 0, 0)),
            _const_spec((d, d)),
            _const_spec((1, d)),
            _const_spec((n_chunks, d, FFN_CHUNK)),
            _const_spec((n_chunks, d, FFN_CHUNK)),
            _const_spec((n_chunks, FFN_CHUNK, d)),
        ],
        out_specs=pl.BlockSpec((1, rows, d), lambda b, s: (b, s, 0)),
        out_shape=jax.ShapeDtypeStruct((bsz, seq, d), jnp.float32),
        scratch_shapes=[pltpu.VMEM((rows, d), jnp.float32)],
        compiler_params=pltpu.CompilerParams(
            dimension_semantics=("arbitrary", "arbitrary"),
            vmem_limit_bytes=V7X_VMEM_LIMIT_BYTES),
        name="proj_ffn",
    )(x, y, mod, w_out.astype(jnp.bfloat16), ffn_norm.reshape(1, d), wg, wu, wd)


def _qkv_kernel(x_ref, mod_ref, norm_ref, w_ref, pool_ref, qn_ref, kn_ref,
                q_ref, k_ref, v_ref):
    d = x_ref.shape[2]
    x = x_ref[0]
    h = _rms_norm_modulate(x, norm_ref[...], mod_ref[0, 0:1, :], mod_ref[0, 1:2, :])
    qkv = _dot(h.astype(jnp.bfloat16), w_ref[...])
    q, k, v = qkv[:, 0:d], qkv[:, d:2 * d], qkv[:, 2 * d:3 * d]

    def head_norm(t, g):
        t_hi, t_lo = _split_bf16(t * t)
        ms = _dot(t_hi, pool_ref[...]) + _dot(t_lo, pool_ref[...])
        return t * lax.rsqrt(ms + EPS) * g

    q_ref[0] = (head_norm(q, qn_ref[...]) * (HEAD_DIM ** -0.5)).astype(q_ref.dtype)
    k_ref[0] = head_norm(k, kn_ref[...]).astype(k_ref.dtype)
    v_ref[0] = v.astype(v_ref.dtype)


def _qkv(x, mod, mix_norm, w_qkv, q_norm, k_norm):
    bsz, seq, d = x.shape
    rows = QKV_ROWS
    assert seq % rows == 0
    n_heads = d // HEAD_DIM
    head_of_lane = jnp.arange(d) // HEAD_DIM
    pool = (head_of_lane[:, None] == head_of_lane[None, :]).astype(jnp.bfloat16) * (1.0 / HEAD_DIM)
    tile = pl.BlockSpec((1, rows, d), lambda b, s: (b, s, 0))
    out = jax.ShapeDtypeStruct((bsz, seq, d), jnp.bfloat16)
    return pl.pallas_call(
        _qkv_kernel,
        grid=(bsz, seq // rows),
        in_specs=[
            tile,
            pl.BlockSpec((1, N_MOD, d), lambda b, s: (b, 0, 0)),
            _const_spec((1, d)),
            _const_spec((d, 3 * d)),
            _const_spec((d, d)),
            _const_spec((1, d)),
            _const_spec((1, d)),
        ],
        out_specs=[tile, tile, tile],
        out_shape=[out, out, out],
        compiler_params=pltpu.CompilerParams(
            dimension_semantics=("arbitrary", "arbitrary"),
            vmem_limit_bytes=V7X_VMEM_LIMIT_BYTES),
        name="l1_qkv",
    )(x, mod, mix_norm.reshape(1, d), w_qkv.astype(jnp.bfloat16), pool.astype(jnp.bfloat16),
      jnp.tile(q_norm, n_heads).reshape(1, d), jnp.tile(k_norm, n_heads).reshape(1, d))


def _attention_kernel(q_ref, k_ref, v_ref, o_ref):
    blk = q_ref.shape[1]
    qi = pl.program_id(2)
    q = q_ref[0]
    lane = lax.broadcasted_iota(jnp.int32, q.shape, 1)
    zero = jnp.zeros_like(q)
    q_heads = [jnp.where((lane >= h * HEAD_DIM) & (lane < (h + 1) * HEAD_DIM), q, zero)
               for h in range(HEADS_PER_LANE_GROUP)]

    row = lax.broadcasted_iota(jnp.int32, (blk, blk), 0)
    col = lax.broadcasted_iota(jnp.int32, (blk, blk), 1)
    later = jnp.where(row > col, 1.0, 0.0).astype(jnp.bfloat16)
    strict = col < row

    def block_update(j, carry, diagonal):
        start = pl.multiple_of(j * blk, blk)
        k = k_ref[0, pl.ds(start, blk), :]
        v = v_ref[0, pl.ds(start, blk), :]
        new = []
        for qh, (c, acc) in zip(q_heads, carry):
            z = lax.dot_general(qh, k, (((1,), (1,)), ((), ())),
                                preferred_element_type=jnp.float32)
            sp = _softplus(z)
            if diagonal:
                sp = jnp.where(strict, sp, 0.0)
            suffix = _dot(sp.astype(jnp.bfloat16), later)
            w = jnp.exp(z - sp - suffix - c)
            if diagonal:
                w = jnp.where(strict, w, 0.0)
            acc = acc + _dot(w.astype(jnp.bfloat16), v)
            c = c + jnp.sum(sp, axis=1, keepdims=True)
            new.append((c, acc))
        return tuple(new)

    init = tuple((jnp.zeros((blk, 1), jnp.float32), jnp.zeros(q.shape, jnp.float32))
                 for _ in q_heads)
    carry = block_update(qi, init, True)
    carry = lax.fori_loop(0, qi, lambda t, cr: block_update(qi - 1 - t, cr, False), carry)

    out = carry[0][1]
    for h in range(1, HEADS_PER_LANE_GROUP):
        out = jnp.where(lane >= h * HEAD_DIM, carry[h][1], out)
    o_ref[0] = out.astype(o_ref.dtype)


def _attention(q, k, v):
    bsz, seq, d = q.shape
    blk = ATT_BLOCK
    assert seq % blk == 0 and d % V7X_LANES == 0
    q_spec = pl.BlockSpec((1, blk, V7X_LANES), lambda b, p, i: (b, i, p))
    kv_spec = pl.BlockSpec((1, seq, V7X_LANES), lambda b, p, i: (b, 0, p))
    return pl.pallas_call(
        _attention_kernel,
        grid=(bsz, d // V7X_LANES, seq // blk),
        in_specs=[q_spec, kv_spec, kv_spec],
        out_specs=q_spec,
        out_shape=jax.ShapeDtypeStruct((bsz, seq, d), jnp.bfloat16),
        compiler_params=pltpu.CompilerParams(
            dimension_semantics=("arbitrary", "arbitrary", "arbitrary"),
            vmem_limit_bytes=V7X_VMEM_LIMIT_BYTES),
        name="sb_attention",
    )(q, k, v)


def kernel(x, c, l0_ada_w, l0_ada_b, l0_mix_norm, l0_w_in, l0_conv_a_w, l0_conv_b_w, l0_conv_b_b, l0_rg_a_w, l0_rg_a_b, l0_rg_x_w, l0_rg_x_b, l0_rg_lambda, l0_w_out, l0_ffn_norm, l0_ffn_w_gate, l0_ffn_w_up, l0_ffn_w_down, l1_ada_w, l1_ada_b, l1_mix_norm, l1_w_qkv, l1_q_norm, l1_k_norm, l1_w_out, l1_ffn_norm, l1_ffn_w_gate, l1_ffn_w_up, l1_ffn_w_down):
    mod0 = _ada_modulation(c, l0_ada_w, l0_ada_b)
    mod1 = _ada_modulation(c, l1_ada_w, l1_ada_b)

    y = _mixer(x, mod0, l0_mix_norm, l0_w_in, l0_conv_a_w, l0_conv_b_w, l0_conv_b_b,
               l0_rg_a_w, l0_rg_a_b, l0_rg_x_w, l0_rg_x_b, l0_rg_lambda)
    x = _proj_ffn(x, y, mod0, l0_w_out, l0_ffn_norm, l0_ffn_w_gate, l0_ffn_w_up, l0_ffn_w_down)

    q, k, v = _qkv(x, mod1, l1_mix_norm, l1_w_qkv, l1_q_norm, l1_k_norm)
    o = _attention(q, k, v)
    x = _proj_ffn(x, o, mod1, l1_w_out, l1_ffn_norm, l1_ffn_w_gate, l1_ffn_w_up, l1_ffn_w_down)
    return x
```

```python
import functools
import math

import jax
import jax.numpy as jnp
from jax import lax
from jax.experimental import pallas as pl
from jax.experimental.pallas import tpu as pltpu

D_MODEL = 1024
HEAD_DIM = 64
CONV_WIDTH = D_MODEL // 2
LRU_WIDTH = D_MODEL // 2
SHORT_CONV_K = 3
LRU_CONV_K = 4
RG_C = 8.0
N_MOD = 6
EPS = 1e-6
LOG2E = math.log2(math.e)

V7X_LANES = 128
V7X_SUBLANES = 8
V7X_MXU_DIM = 256
V7X_VMEM_LIMIT_BYTES = 56 * 1024 * 1024

HEADS_PER_LANE_GROUP = V7X_LANES // HEAD_DIM

MIX_ROWS = 256
FFN_ROWS = 512
QKV_ROWS = 512
ATT_BLOCK = 256
FFN_CHUNK = V7X_MXU_DIM
ADA_COLS = 1536


def _const_spec(shape):
    zeros = (0,) * len(shape)
    return pl.BlockSpec(shape, lambda *_: zeros, pipeline_mode=pl.Buffered(1))


def _sigmoid(x):
    return 1.0 / (1.0 + jnp.exp(-x))


def _softplus(x):
    return jnp.maximum(x, 0.0) + jnp.log(1.0 + jnp.exp(-jnp.abs(x)))


def _gelu_tanh(x):
    c = math.sqrt(2.0 / math.pi)
    return 0.5 * x * (1.0 + jnp.tanh(c * (x + 0.044715 * (x * x * x))))


def _rms_norm_modulate(x, g, shift, scale):
    ms = jnp.mean(x * x, axis=-1, keepdims=True)
    y = x * lax.rsqrt(ms + EPS) * g
    return y * (1.0 + scale) + shift


def _split_bf16(x):
    hi = x.astype(jnp.bfloat16)
    lo = (x - hi.astype(jnp.float32)).astype(jnp.bfloat16)
    return hi, lo


def _dot(a, b):
    return jnp.dot(a, b, preferred_element_type=jnp.float32)


def _ada_kernel(c_ref, w_ref, b_ref, o_ref):
    c = c_ref[...]
    s = c * _sigmoid(c)
    s_hi, s_lo = _split_bf16(s)
    w_hi, w_lo = _split_bf16(w_ref[...])
    acc = _dot(s_hi, w_hi) + _dot(s_hi, w_lo) + _dot(s_lo, w_hi)
    o_ref[...] = acc + b_ref[...]


def _ada_modulation(c, ada_w, ada_b):
    bsz, d = c.shape
    n = ada_w.shape[1]
    assert n % ADA_COLS == 0
    out = pl.pallas_call(
        _ada_kernel,
        grid=(n // ADA_COLS,),
        in_specs=[
            pl.BlockSpec((bsz, d), lambda j: (0, 0)),
            pl.BlockSpec((d, ADA_COLS), lambda j: (0, j)),
            pl.BlockSpec((1, ADA_COLS), lambda j: (0, j)),
        ],
        out_specs=pl.BlockSpec((bsz, ADA_COLS), lambda j: (0, j)),
        out_shape=jax.ShapeDtypeStruct((bsz, n), jnp.float32),
        compiler_params=pltpu.CompilerParams(dimension_semantics=("arbitrary",)),
        name="ada_modulation",
    )(c, ada_w, ada_b.reshape(1, n))
    return out.reshape(bsz, N_MOD, d)


def _shift_rows(x, d, fill, row):
    return jnp.where(row >= d, pltpu.roll(x, d, 0), fill)


def _linear_scan(a, b):
    n = a.shape[0]
    row = lax.broadcasted_iota(jnp.int32, a.shape, 0)
    d = 1
    while d < n:
        a_prev = _shift_rows(a, d, 1.0, row)
        b_prev = _shift_rows(b, d, 0.0, row)
        b = a * b_prev + b
        a = a * a_prev
        d *= 2
    return a, b


def _mixer_kernel(x_ref, mod_ref, norm_ref, w_in_ref, conv_a_ref, conv_b_ref, conv_bb_ref,
                  wa_ref, ba_ref, wx_ref, bx_ref, lam_ref, y_ref,
                  p_hist, r_hist, h_state):
    rows = x_ref.shape[1]
    hist = V7X_SUBLANES
    first = pl.program_id(1) == 0

    @pl.when(first)
    def _():
        p_hist[0:hist, :] = jnp.zeros((hist, CONV_WIDTH), jnp.float32)
        r_hist[0:hist, :] = jnp.zeros((hist, LRU_WIDTH), jnp.float32)
        h_state[...] = jnp.zeros_like(h_state)

    @pl.when(jnp.logical_not(first))
    def _():
        p_hist[0:hist, :] = p_hist[rows:rows + hist, :]
        r_hist[0:hist, :] = r_hist[rows:rows + hist, :]

    x = x_ref[0]
    h = _rms_norm_modulate(x, norm_ref[...], mod_ref[0, 0:1, :], mod_ref[0, 1:2, :])
    u = _dot(h.astype(jnp.bfloat16), w_in_ref[...])
    w = CONV_WIDTH
    a_b, a_c, a_x = u[:, 0:w], u[:, w:2 * w], u[:, 2 * w:3 * w]
    r_gate, r_x = u[:, 3 * w:4 * w], u[:, 4 * w:5 * w]

    p = a_c * a_x
    p_hist[hist:hist + rows, :] = p
    conv = conv_a_ref[SHORT_CONV_K - 1:SHORT_CONV_K, :] * p
    for k in range(SHORT_CONV_K - 1):
        back = SHORT_CONV_K - 1 - k
        conv = conv + conv_a_ref[k:k + 1, :] * p_hist[pl.ds(hist - back, rows), :]
    y_ref[0, :, 0:w] = (a_b * conv).astype(y_ref.dtype)

    r_hist[hist:hist + rows, :] = r_x
    xr = conv_b_ref[LRU_CONV_K - 1:LRU_CONV_K, :] * r_x + conv_bb_ref[...]
    for k in range(LRU_CONV_K - 1):
        back = LRU_CONV_K - 1 - k
        xr = xr + conv_b_ref[k:k + 1, :] * r_hist[pl.ds(hist - back, rows), :]
    xr16 = xr.astype(jnp.bfloat16)
    r = _sigmoid(_dot(xr16, wa_ref[...]) + ba_ref[...])
    i = _sigmoid(_dot(xr16, wx_ref[...]) + bx_ref[...])
    log_a = (-RG_C) * r * _softplus(-lam_ref[...])
    a = jnp.exp(log_a)
    b = jnp.sqrt(1.0 - a * a) * (i * xr)
    a_cum, b_cum = _linear_scan(a, b)
    hs = a_cum * h_state[0:1, :] + b_cum
    h_state[0:1, :] = hs[rows - 1:rows, :]
    y_ref[0, :, w:2 * w] = (_gelu_tanh(r_gate) * hs).astype(y_ref.dtype)


def _block_diag(w):
    n_blk, blk, _ = w.shape
    eye = jnp.eye(n_blk, dtype=w.dtype)
    return jnp.einsum('hij,hg->higj', w, eye).reshape(n_blk * blk, n_blk * blk)


def _mixer(x, mod, mix_norm, w_in, conv_a_w, conv_b_w, conv_b_b,
           rg_a_w, rg_a_b, rg_x_w, rg_x_b, rg_lambda):
    bsz, seq, d = x.shape
    rows = MIX_ROWS
    assert seq % rows == 0
    n_in = w_in.shape[1]
    w = LRU_WIDTH
    return pl.pallas_call(
        _mixer_kernel,
        grid=(bsz, seq // rows),
        in_specs=[
            pl.BlockSpec((1, rows, d), lambda b, s: (b, s, 0)),
            pl.BlockSpec((1, N_MOD, d), lambda b, s: (b, 0, 0)),
            _const_spec((1, d)),
            _const_spec((d, n_in)),
            _const_spec((SHORT_CONV_K, CONV_WIDTH)),
            _const_spec((LRU_CONV_K, w)),
            _const_spec((1, w)),
            _const_spec((w, w)),
            _const_spec((1, w)),
            _const_spec((w, w)),
            _const_spec((1, w)),
            _const_spec((1, w)),
        ],
        out_specs=pl.BlockSpec((1, rows, d), lambda b, s: (b, s, 0)),
        out_shape=jax.ShapeDtypeStruct((bsz, seq, d), jnp.bfloat16),
        scratch_shapes=[
            pltpu.VMEM((V7X_SUBLANES + rows, CONV_WIDTH), jnp.float32),
            pltpu.VMEM((V7X_SUBLANES + rows, w), jnp.float32),
            pltpu.VMEM((V7X_SUBLANES, w), jnp.float32),
        ],
        compiler_params=pltpu.CompilerParams(
            dimension_semantics=("arbitrary", "arbitrary"),
            vmem_limit_bytes=V7X_VMEM_LIMIT_BYTES),
        name="l0_mixer",
    )(x, mod, mix_norm.reshape(1, d), w_in.astype(jnp.bfloat16), conv_a_w, conv_b_w,
      conv_b_b.reshape(1, w),
      _block_diag(rg_a_w).astype(jnp.bfloat16), rg_a_b.reshape(1, w),
      _block_diag(rg_x_w).astype(jnp.bfloat16), rg_x_b.reshape(1, w),
      rg_lambda.reshape(1, w))


def _proj_ffn_kernel(x_ref, y_ref, mod_ref, w_out_ref, norm_ref, wg_ref, wu_ref, wd_ref,
                     o_ref, acc_ref):
    x = x_ref[0]
    x1 = x + mod_ref[0, 2:3, :] * _dot(y_ref[0], w_out_ref[...])
    hf = _rms_norm_modulate(x1, norm_ref[...], mod_ref[0, 3:4, :], mod_ref[0, 4:5, :])
    hf16 = hf.astype(jnp.bfloat16)
    n_chunks = wg_ref.shape[0]
    for c in range(n_chunks):
        g = _dot(hf16, wg_ref[c])
        up = _dot(hf16, wu_ref[c])
        act = (g * _sigmoid(g) * up).astype(jnp.bfloat16)
        part = _dot(act, wd_ref[c])
        if c == 0:
            acc_ref[...] = part
        else:
            acc_ref[...] += part
    o_ref[0] = x1 + mod_ref[0, 5:6, :] * acc_ref[...]


def _proj_ffn(x, y, mod, w_out, ffn_norm, w_gate, w_up, w_down):
    bsz, seq, d = x.shape
    rows = FFN_ROWS
    d_ff = w_gate.shape[1]
    assert seq % rows == 0 and d_ff % FFN_CHUNK == 0
    n_chunks = d_ff // FFN_CHUNK
    wg = w_gate.astype(jnp.bfloat16).reshape(d, n_chunks, FFN_CHUNK).transpose(1, 0, 2)
    wu = w_up.astype(jnp.bfloat16).reshape(d, n_chunks, FFN_CHUNK).transpose(1, 0, 2)
    wd = w_down.astype(jnp.bfloat16).reshape(n_chunks, FFN_CHUNK, d)
    return pl.pallas_call(
        _proj_ffn_kernel,
        grid=(bsz, seq // rows),
        in_specs=[
            pl.BlockSpec((1, rows, d), lambda b, s: (b, s, 0)),
            pl.BlockSpec((1, rows, d), lambda b, s: (b, s, 0)),
            pl.BlockSpec((1, N_MOD, d), lambda b, s: (b, 0, 0)),
            _const_spec((d, d)),
            _const_spec((1, d)),
            _const_spec((n_chunks, d, FFN_CHUNK)),
            _const_spec((n_chunks, d, FFN_CHUNK)),
            _const_spec((n_chunks, FFN_CHUNK, d)),
        ],
        out_specs=pl.BlockSpec((1, rows, d), lambda b, s: (b, s, 0)),
        out_shape=jax.ShapeDtypeStruct((bsz, seq, d), jnp.float32),
        scratch_shapes=[pltpu.VMEM((rows, d), jnp.float32)],
        compiler_params=pltpu.CompilerParams(
            dimension_semantics=("arbitrary", "arbitrary"),
            vmem_limit_bytes=V7X_VMEM_LIMIT_BYTES),
        name="proj_ffn",
    )(x, y, mod, w_out.astype(jnp.bfloat16), ffn_norm.reshape(1, d), wg, wu, wd)


def _qkv_kernel(x_ref, mod_ref, norm_ref, w_ref, pool_ref, qn_ref, kn_ref,
                q_ref, k_ref, v_ref):
    d = x_ref.shape[2]
    x = x_ref[0]
    h = _rms_norm_modulate(x, norm_ref[...], mod_ref[0, 0:1, :], mod_ref[0, 1:2, :])
    qkv = _dot(h.astype(jnp.bfloat16), w_ref[...])
    q, k, v = qkv[:, 0:d], qkv[:, d:2 * d], qkv[:, 2 * d:3 * d]

    def head_norm(t, g):
        t_hi, t_lo = _split_bf16(t * t)
        ms = _dot(t_hi, pool_ref[...]) + _dot(t_lo, pool_ref[...])
        return t * lax.rsqrt(ms + EPS) * g

    q_ref[0] = (head_norm(q, qn_ref[...]) * (LOG2E * HEAD_DIM ** -0.5)).astype(q_ref.dtype)
    k_ref[0] = head_norm(k, kn_ref[...]).astype(k_ref.dtype)
    v_ref[0] = v.astype(v_ref.dtype)


def _qkv(x, mod, mix_norm, w_qkv, q_norm, k_norm):
    bsz, seq, d = x.shape
    rows = QKV_ROWS
    assert seq % rows == 0
    n_heads = d // HEAD_DIM
    head_of_lane = jnp.arange(d) // HEAD_DIM
    pool = (head_of_lane[:, None] == head_of_lane[None, :]).astype(jnp.bfloat16) * (1.0 / HEAD_DIM)
    tile = pl.BlockSpec((1, rows, d), lambda b, s: (b, s, 0))
    out = jax.ShapeDtypeStruct((bsz, seq, d), jnp.bfloat16)
    return pl.pallas_call(
        _qkv_kernel,
        grid=(bsz, seq // rows),
        in_specs=[
            tile,
            pl.BlockSpec((1, N_MOD, d), lambda b, s: (b, 0, 0)),
            _const_spec((1, d)),
            _const_spec((d, 3 * d)),
            _const_spec((d, d)),
            _const_spec((1, d)),
            _const_spec((1, d)),
        ],
        out_specs=[tile, tile, tile],
        out_shape=[out, out, out],
        compiler_params=pltpu.CompilerParams(
            dimension_semantics=("arbitrary", "arbitrary"),
            vmem_limit_bytes=V7X_VMEM_LIMIT_BYTES),
        name="l1_qkv",
    )(x, mod, mix_norm.reshape(1, d), w_qkv.astype(jnp.bfloat16), pool.astype(jnp.bfloat16),
      jnp.tile(q_norm, n_heads).reshape(1, d), jnp.tile(k_norm, n_heads).reshape(1, d))


MASK_BIAS = -1e30
N_STAGES = 3


def _attention_items(n_blocks):
    items = [(i, j) for i in range(n_blocks) for j in range(i, -1, -1)]
    pad = [(0, 0)] * N_STAGES
    return items, jnp.asarray(list(zip(*(items + pad))), dtype=jnp.int32)


def _attention_kernel(tab_ref, q_ref, k_ref, v_ref, later_ref, bias_ref, o_ref,
                      q_heads, acc, cs, sp16, m32, w16, rs, *, n_items):
    blk = ATT_BLOCK
    n_heads = HEADS_PER_LANE_GROUP

    q = q_ref[0]
    lane = lax.broadcasted_iota(jnp.int32, q.shape, 1)
    for h in range(n_heads):
        in_head = (lane >= h * HEAD_DIM) & (lane < (h + 1) * HEAD_DIM)
        q_heads[h] = jnp.where(in_head, q, jnp.zeros_like(q))

    def item(n):
        qi, kj = tab_ref[0, n], tab_ref[1, n]
        return (pl.multiple_of(qi * blk, blk), pl.multiple_of(kj * blk, blk), qi == kj)

    def stage1(n, slot):
        q_row, k_row, diag = item(n)
        k = k_ref[0, pl.ds(k_row, blk), :]
        bias = bias_ref[diag.astype(jnp.int32)]
        for h in range(n_heads):
            z = lax.dot_general(q_heads[h, pl.ds(q_row, blk), :], k, (((1,), (1,)), ((), ())),
                                preferred_element_type=jnp.float32) + bias
            sp = jnp.maximum(z, 0.0) + jnp.log(1.0 + jnp.exp2(-jnp.abs(z))) * LOG2E
            m32[slot, h] = z - sp
            sp16[slot, h] = sp.astype(jnp.bfloat16)
            rs[slot, h] = jnp.sum(sp, axis=1, keepdims=True)

    def stage2(n, slot):
        q_row, _, diag = item(n)
        for h in range(n_heads):
            suffix = _dot(sp16[slot, h], later_ref[...])
            c = jnp.where(diag, 0.0, cs[h, pl.ds(q_row, blk), :])
            w16[slot, h] = jnp.exp2(m32[slot, h] - suffix - c).astype(jnp.bfloat16)
            cs[h, pl.ds(q_row, blk), :] = c + rs[slot, h]

    def stage3(n, slot):
        q_row, k_row, diag = item(n)
        v = v_ref[0, pl.ds(k_row, blk), :]
        for h in range(n_heads):
            prev = jnp.where(diag, 0.0, acc[h, pl.ds(q_row, blk), :])
            acc[h, pl.ds(q_row, blk), :] = prev + _dot(w16[slot, h], v)

    def step(it, slot, stages):
        if 1 in stages:
            stage1(it, slot)
        if 2 in stages:
            stage2(it - 1, 1 - slot)
        if 3 in stages:
            stage3(it - 2, slot)

    assert n_items % 2 == 0
    step(0, 0, (1,))
    step(1, 1, (1, 2))

    def pair(t, carry):
        it = 2 * t + 2
        step(it, 0, (1, 2, 3))
        step(it + 1, 1, (1, 2, 3))
        return carry

    lax.fori_loop(0, (n_items - 2) // 2, pair, 0)
    step(n_items, 0, (2, 3))
    step(n_items + 1, 1, (3,))

    out = acc[0]
    for h in range(1, n_heads):
        out = jnp.where(lane >= h * HEAD_DIM, acc[h], out)
    o_ref[0] = out.astype(o_ref.dtype)


def _attention(q, k, v):
    bsz, seq, d = q.shape
    blk = ATT_BLOCK
    n_heads = HEADS_PER_LANE_GROUP
    assert seq % blk == 0 and d % V7X_LANES == 0
    items, table = _attention_items(seq // blk)
    idx = jnp.arange(blk)
    later = (idx[:, None] > idx[None, :]).astype(jnp.bfloat16)
    causal = jnp.where(idx[None, :] < idx[:, None], 0.0, MASK_BIAS)
    bias = jnp.stack([jnp.zeros((blk, blk), jnp.float32), causal.astype(jnp.float32)])
    seq_spec = pl.BlockSpec((1, seq, V7X_LANES), lambda b, p, tab: (b, 0, p))
    grid_spec = pltpu.PrefetchScalarGridSpec(
        num_scalar_prefetch=1,
        grid=(bsz, d // V7X_LANES),
        in_specs=[
            seq_spec, seq_spec, seq_spec,
            pl.BlockSpec((blk, blk), lambda b, p, tab: (0, 0), pipeline_mode=pl.Buffered(1)),
            pl.BlockSpec((2, blk, blk), lambda b, p, tab: (0, 0, 0), pipeline_mode=pl.Buffered(1)),
        ],
        out_specs=seq_spec,
        scratch_shapes=[
            pltpu.VMEM((n_heads, seq, V7X_LANES), jnp.bfloat16),
            pltpu.VMEM((n_heads, seq, V7X_LANES), jnp.float32),
            pltpu.VMEM((n_heads, seq, 1), jnp.float32),
            pltpu.VMEM((2, n_heads, blk, blk), jnp.bfloat16),
            pltpu.VMEM((2, n_heads, blk, blk), jnp.float32),
            pltpu.VMEM((2, n_heads, blk, blk), jnp.bfloat16),
            pltpu.VMEM((2, n_heads, blk, 1), jnp.float32),
        ],
    )
    return pl.pallas_call(
        functools.partial(_attention_kernel, n_items=len(items)),
        grid_spec=grid_spec,
        out_shape=jax.ShapeDtypeStruct((bsz, seq, d), jnp.bfloat16),
        compiler_params=pltpu.CompilerParams(
            dimension_semantics=("arbitrary", "arbitrary"),
            vmem_limit_bytes=V7X_VMEM_LIMIT_BYTES),
        name="sb_attention",
    )(table, q, k, v, later, bias)


def kernel(x, c, l0_ada_w, l0_ada_b, l0_mix_norm, l0_w_in, l0_conv_a_w, l0_conv_b_w, l0_conv_b_b, l0_rg_a_w, l0_rg_a_b, l0_rg_x_w, l0_rg_x_b, l0_rg_lambda, l0_w_out, l0_ffn_norm, l0_ffn_w_gate, l0_ffn_w_up, l0_ffn_w_down, l1_ada_w, l1_ada_b, l1_mix_norm, l1_w_qkv, l1_q_norm, l1_k_norm, l1_w_out, l1_ffn_norm, l1_ffn_w_gate, l1_ffn_w_up, l1_ffn_w_down):
    mod0 = _ada_modulation(c, l0_ada_w, l0_ada_b)
    mod1 = _ada_modulation(c, l1_ada_w, l1_ada_b)

    y = _mixer(x, mod0, l0_mix_norm, l0_w_in, l0_conv_a_w, l0_conv_b_w, l0_conv_b_b,
               l0_rg_a_w, l0_rg_a_b, l0_rg_x_w, l0_rg_x_b, l0_rg_lambda)
    x = _proj_ffn(x, y, mod0, l0_w_out, l0_ffn_norm, l0_ffn_w_gate, l0_ffn_w_up, l0_ffn_w_down)

    q, k, v = _qkv(x, mod1, l1_mix_norm, l1_w_qkv, l1_q_norm, l1_k_norm)
    o = _attention(q, k, v)
    x = _proj_ffn(x, o, mod1, l1_w_out, l1_ffn_norm, l1_ffn_w_gate, l1_ffn_w_up, l1_ffn_w_down)
    return x
```

```python
import functools
import math

import jax
import jax.numpy as jnp
from jax import lax
from jax.experimental import pallas as pl
from jax.experimental.pallas import tpu as pltpu

D_MODEL = 1024
HEAD_DIM = 64
CONV_WIDTH = D_MODEL // 2
LRU_WIDTH = D_MODEL // 2
SHORT_CONV_K = 3
LRU_CONV_K = 4
RG_C = 8.0
N_MOD = 6
EPS = 1e-6
LOG2E = math.log2(math.e)

V7X_LANES = 128
V7X_SUBLANES = 8
V7X_MXU_DIM = 256
V7X_VMEM_LIMIT_BYTES = 56 * 1024 * 1024

HEADS_PER_LANE_GROUP = V7X_LANES // HEAD_DIM

MIX_ROWS = 256
FFN_ROWS = 512
QKV_ROWS = 512
ATT_BLOCK = 256
FFN_CHUNK = V7X_MXU_DIM
ADA_COLS = 1536


def _const_spec(shape):
    zeros = (0,) * len(shape)
    return pl.BlockSpec(shape, lambda *_: zeros, pipeline_mode=pl.Buffered(1))


def _sigmoid(x):
    return 1.0 / (1.0 + jnp.exp(-x))


def _softplus(x):
    return jnp.maximum(x, 0.0) + jnp.log(1.0 + jnp.exp(-jnp.abs(x)))


def _gelu_tanh(x):
    c = math.sqrt(2.0 / math.pi)
    return 0.5 * x * (1.0 + jnp.tanh(c * (x + 0.044715 * (x * x * x))))


def _rms_norm_modulate(x, g, shift, scale):
    ms = jnp.mean(x * x, axis=-1, keepdims=True)
    y = x * lax.rsqrt(ms + EPS) * g
    return y * (1.0 + scale) + shift


def _split_bf16(x):
    hi = x.astype(jnp.bfloat16)
    lo = (x - hi.astype(jnp.float32)).astype(jnp.bfloat16)
    return hi, lo


def _dot(a, b):
    return jnp.dot(a, b, preferred_element_type=jnp.float32)


def _ada_kernel(c_ref, w_ref, b_ref, o_ref):
    c = c_ref[...]
    s = c * _sigmoid(c)
    s_hi, s_lo = _split_bf16(s)
    w_hi, w_lo = _split_bf16(w_ref[...])
    acc = _dot(s_hi, w_hi) + _dot(s_hi, w_lo) + _dot(s_lo, w_hi)
    o_ref[...] = acc + b_ref[...]


def _ada_modulation(c, ada_w, ada_b):
    bsz, d = c.shape
    n = ada_w.shape[1]
    assert n % ADA_COLS == 0
    out = pl.pallas_call(
        _ada_kernel,
        grid=(n // ADA_COLS,),
        in_specs=[
            pl.BlockSpec((bsz, d), lambda j: (0, 0)),
            pl.BlockSpec((d, ADA_COLS), lambda j: (0, j)),
            pl.BlockSpec((1, ADA_COLS), lambda j: (0, j)),
        ],
        out_specs=pl.BlockSpec((bsz, ADA_COLS), lambda j: (0, j)),
        out_shape=jax.ShapeDtypeStruct((bsz, n), jnp.float32),
        compiler_params=pltpu.CompilerParams(dimension_semantics=("arbitrary",)),
        name="ada_modulation",
    )(c, ada_w, ada_b.reshape(1, n))
    return out.reshape(bsz, N_MOD, d)


def _shift_rows(x, d, fill, row):
    return jnp.where(row >= d, pltpu.roll(x, d, 0), fill)


def _linear_scan(a, b):
    n = a.shape[0]
    row = lax.broadcasted_iota(jnp.int32, a.shape, 0)
    d = 1
    while d < n:
        a_prev = _shift_rows(a, d, 1.0, row)
        b_prev = _shift_rows(b, d, 0.0, row)
        b = a * b_prev + b
        a = a * a_prev
        d *= 2
    return a, b


def _mixer_kernel(x_ref, mod_ref, norm_ref, w_in_ref, conv_a_ref, conv_b_ref, conv_bb_ref,
                  wa_ref, ba_ref, wx_ref, bx_ref, lam_ref, y_ref,
                  p_hist, r_hist, h_state):
    rows = x_ref.shape[1]
    hist = V7X_SUBLANES
    first = pl.program_id(1) == 0

    @pl.when(first)
    def _():
        p_hist[0:hist, :] = jnp.zeros((hist, CONV_WIDTH), jnp.float32)
        r_hist[0:hist, :] = jnp.zeros((hist, LRU_WIDTH), jnp.float32)
        h_state[...] = jnp.zeros_like(h_state)

    @pl.when(jnp.logical_not(first))
    def _():
        p_hist[0:hist, :] = p_hist[rows:rows + hist, :]
        r_hist[0:hist, :] = r_hist[rows:rows + hist, :]

    x = x_ref[0]
    h = _rms_norm_modulate(x, norm_ref[...], mod_ref[0, 0:1, :], mod_ref[0, 1:2, :])
    u = _dot(h.astype(jnp.bfloat16), w_in_ref[...])
    w = CONV_WIDTH
    a_b, a_c, a_x = u[:, 0:w], u[:, w:2 * w], u[:, 2 * w:3 * w]
    r_gate, r_x = u[:, 3 * w:4 * w], u[:, 4 * w:5 * w]

    p = a_c * a_x
    p_hist[hist:hist + rows, :] = p
    conv = conv_a_ref[SHORT_CONV_K - 1:SHORT_CONV_K, :] * p
    for k in range(SHORT_CONV_K - 1):
        back = SHORT_CONV_K - 1 - k
        conv = conv + conv_a_ref[k:k + 1, :] * p_hist[pl.ds(hist - back, rows), :]
    y_ref[0, :, 0:w] = (a_b * conv).astype(y_ref.dtype)

    r_hist[hist:hist + rows, :] = r_x
    xr = conv_b_ref[LRU_CONV_K - 1:LRU_CONV_K, :] * r_x + conv_bb_ref[...]
    for k in range(LRU_CONV_K - 1):
        back = LRU_CONV_K - 1 - k
        xr = xr + conv_b_ref[k:k + 1, :] * r_hist[pl.ds(hist - back, rows), :]
    xr16 = xr.astype(jnp.bfloat16)
    r = _sigmoid(_dot(xr16, wa_ref[...]) + ba_ref[...])
    i = _sigmoid(_dot(xr16, wx_ref[...]) + bx_ref[...])
    log_a = (-RG_C) * r * _softplus(-lam_ref[...])
    a = jnp.exp(log_a)
    b = jnp.sqrt(1.0 - a * a) * (i * xr)
    a_cum, b_cum = _linear_scan(a, b)
    hs = a_cum * h_state[0:1, :] + b_cum
    h_state[0:1, :] = hs[rows - 1:rows, :]
    y_ref[0, :, w:2 * w] = (_gelu_tanh(r_gate) * hs).astype(y_ref.dtype)


def _block_diag(w):
    n_blk, blk, _ = w.shape
    eye = jnp.eye(n_blk, dtype=w.dtype)
    return jnp.einsum('hij,hg->higj', w, eye).reshape(n_blk * blk, n_blk * blk)


def _mixer(x, mod, mix_norm, w_in, conv_a_w, conv_b_w, conv_b_b,
           rg_a_w, rg_a_b, rg_x_w, rg_x_b, rg_lambda):
    bsz, seq, d = x.shape
    rows = MIX_ROWS
    assert seq % rows == 0
    n_in = w_in.shape[1]
    w = LRU_WIDTH
    return pl.pallas_call(
        _mixer_kernel,
        grid=(bsz, seq // rows),
        in_specs=[
            pl.BlockSpec((1, rows, d), lambda b, s: (b, s, 0)),
            pl.BlockSpec((1, N_MOD, d), lambda b, s: (b, 0, 0)),
            _const_spec((1, d)),
            _const_spec((d, n_in)),
            _const_spec((SHORT_CONV_K, CONV_WIDTH)),
            _const_spec((LRU_CONV_K, w)),
            _const_spec((1, w)),
            _const_spec((w, w)),
            _const_spec((1, w)),
            _const_spec((w, w)),
            _const_spec((1, w)),
            _const_spec((1, w)),
        ],
        out_specs=pl.BlockSpec((1, rows, d), lambda b, s: (b, s, 0)),
        out_shape=jax.ShapeDtypeStruct((bsz, seq, d), jnp.bfloat16),
        scratch_shapes=[
            pltpu.VMEM((V7X_SUBLANES + rows, CONV_WIDTH), jnp.float32),
            pltpu.VMEM((V7X_SUBLANES + rows, w), jnp.float32),
            pltpu.VMEM((V7X_SUBLANES, w), jnp.float32),
        ],
        compiler_params=pltpu.CompilerParams(
            dimension_semantics=("arbitrary", "arbitrary"),
            vmem_limit_bytes=V7X_VMEM_LIMIT_BYTES),
        name="l0_mixer",
    )(x, mod, mix_norm.reshape(1, d), w_in.astype(jnp.bfloat16), conv_a_w, conv_b_w,
      conv_b_b.reshape(1, w),
      _block_diag(rg_a_w).astype(jnp.bfloat16), rg_a_b.reshape(1, w),
      _block_diag(rg_x_w).astype(jnp.bfloat16), rg_x_b.reshape(1, w),
      rg_lambda.reshape(1, w))


def _proj_ffn_kernel(x_ref, y_ref, mod_ref, w_out_ref, norm_ref, wg_ref, wu_ref, wd_ref,
                     o_ref, acc_ref):
    x = x_ref[0]
    x1 = x + mod_ref[0, 2:3, :] * _dot(y_ref[0], w_out_ref[...])
    hf = _rms_norm_modulate(x1, norm_ref[...], mod_ref[0, 3:4, :], mod_ref[0, 4:5, :])
    hf16 = hf.astype(jnp.bfloat16)
    n_chunks = wg_ref.shape[0]
    for c in range(n_chunks):
        g = _dot(hf16, wg_ref[c])
        up = _dot(hf16, wu_ref[c])
        act = (g * _sigmoid(g) * up).astype(jnp.bfloat16)
        part = _dot(act, wd_ref[c])
        if c == 0:
            acc_ref[...] = part
        else:
            acc_ref[...] += part
    o_ref[0] = x1 + mod_ref[0, 5:6, :] * acc_ref[...]


def _proj_ffn(x, y, mod, w_out, ffn_norm, w_gate, w_up, w_down):
    bsz, seq, d = x.shape
    rows = FFN_ROWS
    d_ff = w_gate.shape[1]
    assert seq % rows == 0 and d_ff % FFN_CHUNK == 0
    n_chunks = d_ff // FFN_CHUNK
    wg = w_gate.astype(jnp.bfloat16).reshape(d, n_chunks, FFN_CHUNK).transpose(1, 0, 2)
    wu = w_up.astype(jnp.bfloat16).reshape(d, n_chunks, FFN_CHUNK).transpose(1, 0, 2)
    wd = w_down.astype(jnp.bfloat16).reshape(n_chunks, FFN_CHUNK, d)
    return pl.pallas_call(
        _proj_ffn_kernel,
        grid=(bsz, seq // rows),
        in_specs=[
            pl.BlockSpec((1, rows, d), lambda b, s: (b, s, 0)),
            pl.BlockSpec((1, rows, d), lambda b, s: (b, s, 0)),
            pl.BlockSpec((1, N_MOD, d), lambda b, s: (b, 0, 0)),
            _const_spec((d, d)),
            _const_spec((1, d)),
            _const_spec((n_chunks, d, FFN_CHUNK)),
            _const_spec((n_chunks, d, FFN_CHUNK)),
            _const_spec((n_chunks, FFN_CHUNK, d)),
        ],
        out_specs=pl.BlockSpec((1, rows, d), lambda b, s: (b, s, 0)),
        out_shape=jax.ShapeDtypeStruct((bsz, seq, d), jnp.float32),
        scratch_shapes=[pltpu.VMEM((rows, d), jnp.float32)],
        compiler_params=pltpu.CompilerParams(
            dimension_semantics=("arbitrary", "arbitrary"),
            vmem_limit_bytes=V7X_VMEM_LIMIT_BYTES),
        name="proj_ffn",
    )(x, y, mod, w_out.astype(jnp.bfloat16), ffn_norm.reshape(1, d), wg, wu, wd)


def _qkv_kernel(x_ref, mod_ref, norm_ref, w_ref, pool_ref, qn_ref, kn_ref,
                q_ref, k_ref, v_ref):
    d = x_ref.shape[2]
    x = x_ref[0]
    h = _rms_norm_modulate(x, norm_ref[...], mod_ref[0, 0:1, :], mod_ref[0, 1:2, :])
    qkv = _dot(h.astype(jnp.bfloat16), w_ref[...])
    q, k, v = qkv[:, 0:d], qkv[:, d:2 * d], qkv[:, 2 * d:3 * d]

    def head_norm(t, g):
        ms = _dot((t * t).astype(jnp.bfloat16), pool_ref[...])
        return t * lax.rsqrt(ms + EPS) * g

    q_ref[0] = (head_norm(q, qn_ref[...]) * (LOG2E * HEAD_DIM ** -0.5)).astype(q_ref.dtype)
    k_ref[0] = head_norm(k, kn_ref[...]).astype(k_ref.dtype)
    v_ref[0] = v.astype(v_ref.dtype)


def _qkv(x, mod, mix_norm, w_qkv, q_norm, k_norm):
    bsz, seq, d = x.shape
    rows = QKV_ROWS
    assert seq % rows == 0
    n_heads = d // HEAD_DIM
    head_of_lane = jnp.arange(d) // HEAD_DIM
    pool = (head_of_lane[:, None] == head_of_lane[None, :]).astype(jnp.bfloat16) * (1.0 / HEAD_DIM)
    tile = pl.BlockSpec((1, rows, d), lambda b, s: (b, s, 0))
    out = jax.ShapeDtypeStruct((bsz, seq, d), jnp.bfloat16)
    return pl.pallas_call(
        _qkv_kernel,
        grid=(bsz, seq // rows),
        in_specs=[
            tile,
            pl.BlockSpec((1, N_MOD, d), lambda b, s: (b, 0, 0)),
            _const_spec((1, d)),
            _const_spec((d, 3 * d)),
            _const_spec((d, d)),
            _const_spec((1, d)),
            _const_spec((1, d)),
        ],
        out_specs=[tile, tile, tile],
        out_shape=[out, out, out],
        compiler_params=pltpu.CompilerParams(
            dimension_semantics=("arbitrary", "arbitrary"),
            vmem_limit_bytes=V7X_VMEM_LIMIT_BYTES),
        name="l1_qkv",
    )(x, mod, mix_norm.reshape(1, d), w_qkv.astype(jnp.bfloat16), pool.astype(jnp.bfloat16),
      jnp.tile(q_norm, n_heads).reshape(1, d), jnp.tile(k_norm, n_heads).reshape(1, d))


MASK_BIAS = -1e30
SKIP_LOG2 = 150.0
ITEM_PAD = 2


def _attention_kernel(q_ref, k_ref, v_ref, later_ref, bias_ref, o_ref,
                      q_heads, acc, cs, sp16, m32, w16, rs, cmin, items, *, n_blocks):
    blk = ATT_BLOCK
    n_heads = HEADS_PER_LANE_GROUP
    seq = n_blocks * blk
    trash = n_blocks

    q = q_ref[0]
    lane = lax.broadcasted_iota(jnp.int32, q.shape, 1)
    for h in range(n_heads):
        in_head = (lane >= h * HEAD_DIM) & (lane < (h + 1) * HEAD_DIM)
        q_heads[h, 0:seq, :] = jnp.where(in_head, q, jnp.zeros_like(q))
        q_heads[h, seq:seq + blk, :] = jnp.zeros((blk, V7X_LANES), q_heads.dtype)
        acc[h, seq:seq + blk, :] = jnp.zeros((blk, V7X_LANES), jnp.float32)
        cs[h, seq:seq + blk, :] = jnp.zeros((blk, 1), jnp.float32)

    def item(n, dist):
        qi = items[n]
        kj = jnp.minimum(qi, n_blocks - 1) - dist
        return pl.multiple_of(qi * blk, blk), pl.multiple_of(kj * blk, blk), qi

    def stage1(n, slot, dist, diag):
        q_row, k_row, _ = item(n, dist)
        k = k_ref[0, pl.ds(k_row, blk), :]
        for h in range(n_heads):
            z = lax.dot_general(q_heads[h, pl.ds(q_row, blk), :], k, (((1,), (1,)), ((), ())),
                                preferred_element_type=jnp.float32)
            if diag:
                z = z + bias_ref[...]
            sp = jnp.maximum(z, 0.0) + jnp.log(1.0 + jnp.exp2(-jnp.abs(z))) * LOG2E
            m32[slot, h] = z - sp
            sp16[slot, h] = sp.astype(jnp.bfloat16)
            rs[slot, h] = jnp.sum(sp, axis=1, keepdims=True)

    def stage2(n, slot, dist, diag):
        q_row, _, qi = item(n, dist)
        lowest = None
        for h in range(n_heads):
            t = m32[slot, h] - _dot(sp16[slot, h], later_ref[...])
            c = rs[slot, h]
            if not diag:
                prev = cs[h, pl.ds(q_row, blk), :]
                t = t - prev
                c = c + prev
            w16[slot, h] = jnp.exp2(t).astype(jnp.bfloat16)
            cs[h, pl.ds(q_row, blk), :] = c
            low = jnp.min(c, axis=0, keepdims=True)
            lowest = low if lowest is None else jnp.minimum(lowest, low)
        cmin[pl.ds(qi, 1), :] = jnp.broadcast_to(lowest, (1, V7X_LANES))

    def stage3(n, slot, dist, diag):
        q_row, k_row, _ = item(n, dist)
        v = v_ref[0, pl.ds(k_row, blk), :]
        for h in range(n_heads):
            pv = _dot(w16[slot, h], v)
            if diag:
                acc[h, pl.ds(q_row, blk), :] = pv
            else:
                acc[h, pl.ds(q_row, blk), :] += pv

    def run_wave(n_items, dist, diag):
        def step(it, slot, stages):
            if 1 in stages:
                stage1(it, slot, dist, diag)
            if 2 in stages:
                stage2(it - 1, 1 - slot, dist, diag)
            if 3 in stages:
                stage3(it - 2, slot, dist, diag)

        step(0, 0, (1,))
        step(1, 1, (1, 2))

        def pair(t, carry):
            it = 2 * t + 2
            step(it, 0, (1, 2, 3))
            step(it + 1, 1, (1, 2, 3))
            return carry

        lax.fori_loop(0, (n_items - 2) // 2, pair, 0)
        step(n_items, 0, (2, 3))
        step(n_items + 1, 1, (3,))

    def list_active(dist):
        rows = cmin[0:n_blocks, :]
        block = lax.broadcasted_iota(jnp.int32, rows.shape, 0)
        active = (rows < SKIP_LOG2) & (block >= dist)
        weights = jnp.where(active, jnp.exp2(block.astype(jnp.float32)), 0.0)
        mask = (jnp.sum(weights) * (1.0 / V7X_LANES)).astype(jnp.int32)
        count = jnp.int32(0)
        for i in range(n_blocks):
            items[count] = i
            count = count + ((mask >> i) & 1)
        for p in range(ITEM_PAD):
            items[count + p] = trash
        return count

    assert n_blocks % 2 == 0
    for i in range(n_blocks):
        items[i] = i
    run_wave(n_blocks, 0, True)

    def more_waves(state):
        dist, count = state
        run_wave(count + (count & 1), dist, False)
        return dist + 1, list_active(dist + 1)

    lax.while_loop(lambda state: state[1] > 0, more_waves, (jnp.int32(1), list_active(1)))

    out = acc[0, 0:seq, :]
    for h in range(1, n_heads):
        out = jnp.where(lane >= h * HEAD_DIM, acc[h, 0:seq, :], out)
    o_ref[0] = out.astype(o_ref.dtype)


def _attention(q, k, v):
    bsz, seq, d = q.shape
    blk = ATT_BLOCK
    n_heads = HEADS_PER_LANE_GROUP
    assert seq % blk == 0 and d % V7X_LANES == 0
    n_blocks = seq // blk
    rows = seq + blk
    idx = jnp.arange(blk)
    later = (idx[:, None] > idx[None, :]).astype(jnp.bfloat16)
    bias = jnp.where(idx[None, :] < idx[:, None], 0.0, MASK_BIAS).astype(jnp.float32)
    seq_spec = pl.BlockSpec((1, seq, V7X_LANES), lambda b, p: (b, 0, p))
    cmin_rows = pl.cdiv(n_blocks + 1, V7X_SUBLANES) * V7X_SUBLANES
    return pl.pallas_call(
        functools.partial(_attention_kernel, n_blocks=n_blocks),
        grid=(bsz, d // V7X_LANES),
        in_specs=[seq_spec, seq_spec, seq_spec, _const_spec((blk, blk)), _const_spec((blk, blk))],
        out_specs=seq_spec,
        out_shape=jax.ShapeDtypeStruct((bsz, seq, d), jnp.bfloat16),
        scratch_shapes=[
            pltpu.VMEM((n_heads, rows, V7X_LANES), jnp.bfloat16),
            pltpu.VMEM((n_heads, rows, V7X_LANES), jnp.float32),
            pltpu.VMEM((n_heads, rows, 1), jnp.float32),
            pltpu.VMEM((2, n_heads, blk, blk), jnp.bfloat16),
            pltpu.VMEM((2, n_heads, blk, blk), jnp.float32),
            pltpu.VMEM((2, n_heads, blk, blk), jnp.bfloat16),
            pltpu.VMEM((2, n_heads, blk, 1), jnp.float32),
            pltpu.VMEM((cmin_rows, V7X_LANES), jnp.float32),
            pltpu.SMEM((n_blocks + ITEM_PAD,), jnp.int32),
        ],
        compiler_params=pltpu.CompilerParams(
            dimension_semantics=("arbitrary", "arbitrary"),
            vmem_limit_bytes=V7X_VMEM_LIMIT_BYTES),
        name="sb_attention",
    )(q, k, v, later, bias)


def kernel(x, c, l0_ada_w, l0_ada_b, l0_mix_norm, l0_w_in, l0_conv_a_w, l0_conv_b_w, l0_conv_b_b, l0_rg_a_w, l0_rg_a_b, l0_rg_x_w, l0_rg_x_b, l0_rg_lambda, l0_w_out, l0_ffn_norm, l0_ffn_w_gate, l0_ffn_w_up, l0_ffn_w_down, l1_ada_w, l1_ada_b, l1_mix_norm, l1_w_qkv, l1_q_norm, l1_k_norm, l1_w_out, l1_ffn_norm, l1_ffn_w_gate, l1_ffn_w_up, l1_ffn_w_down):
    mod0 = _ada_modulation(c, l0_ada_w, l0_ada_b)
    mod1 = _ada_modulation(c, l1_ada_w, l1_ada_b)

    y = _mixer(x, mod0, l0_mix_norm, l0_w_in, l0_conv_a_w, l0_conv_b_w, l0_conv_b_b,
               l0_rg_a_w, l0_rg_a_b, l0_rg_x_w, l0_rg_x_b, l0_rg_lambda)
    x = _proj_ffn(x, y, mod0, l0_w_out, l0_ffn_norm, l0_ffn_w_gate, l0_ffn_w_up, l0_ffn_w_down)

    q, k, v = _qkv(x, mod1, l1_mix_norm, l1_w_qkv, l1_q_norm, l1_k_norm)
    o = _attention(q, k, v)
    x = _proj_ffn(x, o, mod1, l1_w_out, l1_ffn_norm, l1_ffn_w_gate, l1_ffn_w_up, l1_ffn_w_down)
    return x
```

```python
import functools
import math

import jax
import jax.numpy as jnp
from jax import lax
from jax.experimental import pallas as pl
from jax.experimental.pallas import tpu as pltpu

D_MODEL = 1024
HEAD_DIM = 64
CONV_WIDTH = D_MODEL // 2
LRU_WIDTH = D_MODEL // 2
SHORT_CONV_K = 3
LRU_CONV_K = 4
RG_C = 8.0
N_MOD = 6
EPS = 1e-6
LOG2E = math.log2(math.e)

V7X_LANES = 128
V7X_SUBLANES = 8
V7X_MXU_DIM = 256
V7X_VMEM_LIMIT_BYTES = 56 * 1024 * 1024

HEADS_PER_LANE_GROUP = V7X_LANES // HEAD_DIM

MIX_ROWS = 256
FFN_ROWS = 512
QKV_ROWS = 512
ATT_BLOCK = 256
FFN_CHUNK = V7X_MXU_DIM
ADA_COLS = 1536


def _const_spec(shape):
    zeros = (0,) * len(shape)
    return pl.BlockSpec(shape, lambda *_: zeros, pipeline_mode=pl.Buffered(1))


def _sigmoid(x):
    return 1.0 / (1.0 + jnp.exp(-x))


def _softplus(x):
    return jnp.maximum(x, 0.0) + jnp.log(1.0 + jnp.exp(-jnp.abs(x)))


def _gelu_tanh(x):
    c = math.sqrt(2.0 / math.pi)
    return 0.5 * x * (1.0 + jnp.tanh(c * (x + 0.044715 * (x * x * x))))


def _rms_norm_modulate(x, g, shift, scale):
    ms = jnp.mean(x * x, axis=-1, keepdims=True)
    y = x * lax.rsqrt(ms + EPS) * g
    return y * (1.0 + scale) + shift


def _split_bf16(x):
    hi = x.astype(jnp.bfloat16)
    lo = (x - hi.astype(jnp.float32)).astype(jnp.bfloat16)
    return hi, lo


def _dot(a, b):
    return jnp.dot(a, b, preferred_element_type=jnp.float32)


def _ada_kernel(c_ref, w_ref, b_ref, o_ref):
    c = c_ref[...]
    s = c * _sigmoid(c)
    s_hi, s_lo = _split_bf16(s)
    w_hi, w_lo = _split_bf16(w_ref[...])
    acc = _dot(s_hi, w_hi) + _dot(s_hi, w_lo) + _dot(s_lo, w_hi)
    o_ref[...] = acc + b_ref[...]


def _ada_modulation(c, ada_w, ada_b):
    bsz, d = c.shape
    n = ada_w.shape[1]
    assert n % ADA_COLS == 0
    out = pl.pallas_call(
        _ada_kernel,
        grid=(n // ADA_COLS,),
        in_specs=[
            pl.BlockSpec((bsz, d), lambda j: (0, 0)),
            pl.BlockSpec((d, ADA_COLS), lambda j: (0, j)),
            pl.BlockSpec((1, ADA_COLS), lambda j: (0, j)),
        ],
        out_specs=pl.BlockSpec((bsz, ADA_COLS), lambda j: (0, j)),
        out_shape=jax.ShapeDtypeStruct((bsz, n), jnp.float32),
        compiler_params=pltpu.CompilerParams(dimension_semantics=("arbitrary",)),
        name="ada_modulation",
    )(c, ada_w, ada_b.reshape(1, n))
    return out.reshape(bsz, N_MOD, d)


def _linear_scan(a, b, h0):
    n, width = a.shape
    group = V7X_SUBLANES
    row = lax.broadcasted_iota(jnp.int32, (group, width), 0)
    h = h0
    out = []
    for g in range(n // group):
        ag = a[g * group:(g + 1) * group, :]
        bg = b[g * group:(g + 1) * group, :]
        d = 1
        while d < group:
            a_prev = jnp.where(row >= d, pltpu.roll(ag, d, 0), 1.0)
            b_prev = jnp.where(row >= d, pltpu.roll(bg, d, 0), 0.0)
            bg = ag * b_prev + bg
            ag = ag * a_prev
            d *= 2
        hg = ag * h + bg
        h = hg[group - 1:group, :]
        out.append(hg)
    return jnp.concatenate(out, axis=0)


def _mixer_kernel(x_ref, mod_ref, norm_ref, w_in_ref, conv_a_ref, conv_b_ref, conv_bb_ref,
                  wa_ref, ba_ref, wx_ref, bx_ref, lam_ref, y_ref,
                  p_hist, r_hist, h_state):
    rows = x_ref.shape[1]
    hist = V7X_SUBLANES
    first = pl.program_id(1) == 0

    @pl.when(first)
    def _():
        p_hist[0:hist, :] = jnp.zeros((hist, CONV_WIDTH), jnp.float32)
        r_hist[0:hist, :] = jnp.zeros((hist, LRU_WIDTH), jnp.float32)
        h_state[...] = jnp.zeros_like(h_state)

    @pl.when(jnp.logical_not(first))
    def _():
        p_hist[0:hist, :] = p_hist[rows:rows + hist, :]
        r_hist[0:hist, :] = r_hist[rows:rows + hist, :]

    x = x_ref[0]
    h = _rms_norm_modulate(x, norm_ref[...], mod_ref[0, 0:1, :], mod_ref[0, 1:2, :])
    u = _dot(h.astype(jnp.bfloat16), w_in_ref[...])
    w = CONV_WIDTH
    a_b, a_c, a_x = u[:, 0:w], u[:, w:2 * w], u[:, 2 * w:3 * w]
    r_gate, r_x = u[:, 3 * w:4 * w], u[:, 4 * w:5 * w]

    p = a_c * a_x
    p_hist[hist:hist + rows, :] = p
    conv = conv_a_ref[SHORT_CONV_K - 1:SHORT_CONV_K, :] * p
    for k in range(SHORT_CONV_K - 1):
        back = SHORT_CONV_K - 1 - k
        conv = conv + conv_a_ref[k:k + 1, :] * p_hist[pl.ds(hist - back, rows), :]
    y_ref[0, :, 0:w] = (a_b * conv).astype(y_ref.dtype)

    r_hist[hist:hist + rows, :] = r_x
    xr = conv_b_ref[LRU_CONV_K - 1:LRU_CONV_K, :] * r_x + conv_bb_ref[...]
    for k in range(LRU_CONV_K - 1):
        back = LRU_CONV_K - 1 - k
        xr = xr + conv_b_ref[k:k + 1, :] * r_hist[pl.ds(hist - back, rows), :]
    xr16 = xr.astype(jnp.bfloat16)
    r = _sigmoid(_dot(xr16, wa_ref[...]) + ba_ref[...])
    i = _sigmoid(_dot(xr16, wx_ref[...]) + bx_ref[...])
    log_a = (-RG_C) * r * _softplus(-lam_ref[...])
    a = jnp.exp(log_a)
    b = jnp.sqrt(1.0 - a * a) * (i * xr)
    hs = _linear_scan(a, b, h_state[0:1, :])
    h_state[0:1, :] = hs[rows - 1:rows, :]
    y_ref[0, :, w:2 * w] = (_gelu_tanh(r_gate) * hs).astype(y_ref.dtype)


def _block_diag(w):
    n_blk, blk, _ = w.shape
    eye = jnp.eye(n_blk, dtype=w.dtype)
    return jnp.einsum('hij,hg->higj', w, eye).reshape(n_blk * blk, n_blk * blk)


def _mixer(x, mod, mix_norm, w_in, conv_a_w, conv_b_w, conv_b_b,
           rg_a_w, rg_a_b, rg_x_w, rg_x_b, rg_lambda):
    bsz, seq, d = x.shape
    rows = MIX_ROWS
    assert seq % rows == 0
    n_in = w_in.shape[1]
    w = LRU_WIDTH
    return pl.pallas_call(
        _mixer_kernel,
        grid=(bsz, seq // rows),
        in_specs=[
            pl.BlockSpec((1, rows, d), lambda b, s: (b, s, 0)),
            pl.BlockSpec((1, N_MOD, d), lambda b, s: (b, 0, 0)),
            _const_spec((1, d)),
            _const_spec((d, n_in)),
            _const_spec((SHORT_CONV_K, CONV_WIDTH)),
            _const_spec((LRU_CONV_K, w)),
            _const_spec((1, w)),
            _const_spec((w, w)),
            _const_spec((1, w)),
            _const_spec((w, w)),
            _const_spec((1, w)),
            _const_spec((1, w)),
        ],
        out_specs=pl.BlockSpec((1, rows, d), lambda b, s: (b, s, 0)),
        out_shape=jax.ShapeDtypeStruct((bsz, seq, d), jnp.bfloat16),
        scratch_shapes=[
            pltpu.VMEM((V7X_SUBLANES + rows, CONV_WIDTH), jnp.float32),
            pltpu.VMEM((V7X_SUBLANES + rows, w), jnp.float32),
            pltpu.VMEM((V7X_SUBLANES, w), jnp.float32),
        ],
        compiler_params=pltpu.CompilerParams(
            dimension_semantics=("arbitrary", "arbitrary"),
            vmem_limit_bytes=V7X_VMEM_LIMIT_BYTES),
        name="l0_mixer",
    )(x, mod, mix_norm.reshape(1, d), w_in.astype(jnp.bfloat16), conv_a_w, conv_b_w,
      conv_b_b.reshape(1, w),
      _block_diag(rg_a_w).astype(jnp.bfloat16), rg_a_b.reshape(1, w),
      _block_diag(rg_x_w).astype(jnp.bfloat16), rg_x_b.reshape(1, w),
      rg_lambda.reshape(1, w))


def _proj_ffn_kernel(x_ref, y_ref, mod_ref, w_out_ref, norm_ref, wg_ref, wu_ref, wd_ref,
                     o_ref, acc_ref):
    x = x_ref[0]
    x1 = x + mod_ref[0, 2:3, :] * _dot(y_ref[0], w_out_ref[...])
    hf = _rms_norm_modulate(x1, norm_ref[...], mod_ref[0, 3:4, :], mod_ref[0, 4:5, :])
    hf16 = hf.astype(jnp.bfloat16)
    for c in range(wg_ref.shape[1] // FFN_CHUNK):
        cols = slice(c * FFN_CHUNK, (c + 1) * FFN_CHUNK)
        g = _dot(hf16, wg_ref[:, cols])
        up = _dot(hf16, wu_ref[:, cols])
        act = (g * _sigmoid(g) * up).astype(jnp.bfloat16)
        part = _dot(act, wd_ref[cols, :])
        if c == 0:
            acc_ref[...] = part
        else:
            acc_ref[...] += part
    o_ref[0] = x1 + mod_ref[0, 5:6, :] * acc_ref[...]


def _proj_ffn(x, y, mod, w_out, ffn_norm, w_gate, w_up, w_down):
    bsz, seq, d = x.shape
    rows = FFN_ROWS
    d_ff = w_gate.shape[1]
    assert seq % rows == 0 and d_ff % FFN_CHUNK == 0
    return pl.pallas_call(
        _proj_ffn_kernel,
        grid=(bsz, seq // rows),
        in_specs=[
            pl.BlockSpec((1, rows, d), lambda b, s: (b, s, 0)),
            pl.BlockSpec((1, rows, d), lambda b, s: (b, s, 0)),
            pl.BlockSpec((1, N_MOD, d), lambda b, s: (b, 0, 0)),
            _const_spec((d, d)),
            _const_spec((1, d)),
            _const_spec((d, d_ff)),
            _const_spec((d, d_ff)),
            _const_spec((d_ff, d)),
        ],
        out_specs=pl.BlockSpec((1, rows, d), lambda b, s: (b, s, 0)),
        out_shape=jax.ShapeDtypeStruct((bsz, seq, d), jnp.float32),
        scratch_shapes=[pltpu.VMEM((rows, d), jnp.float32)],
        compiler_params=pltpu.CompilerParams(
            dimension_semantics=("arbitrary", "arbitrary"),
            vmem_limit_bytes=V7X_VMEM_LIMIT_BYTES),
        name="proj_ffn",
    )(x, y, mod, w_out.astype(jnp.bfloat16), ffn_norm.reshape(1, d),
      w_gate.astype(jnp.bfloat16), w_up.astype(jnp.bfloat16), w_down.astype(jnp.bfloat16))


def _qkv_kernel(x_ref, mod_ref, norm_ref, w_ref, pool_ref, qn_ref, kn_ref,
                q_ref, k_ref, v_ref):
    d = x_ref.shape[2]
    x = x_ref[0]
    h = _rms_norm_modulate(x, norm_ref[...], mod_ref[0, 0:1, :], mod_ref[0, 1:2, :])
    qkv = _dot(h.astype(jnp.bfloat16), w_ref[...])
    q, k, v = qkv[:, 0:d], qkv[:, d:2 * d], qkv[:, 2 * d:3 * d]

    def head_norm(t, g):
        ms = _dot((t * t).astype(jnp.bfloat16), pool_ref[...])
        return t * lax.rsqrt(ms + EPS) * g

    q_ref[0] = (head_norm(q, qn_ref[...]) * (LOG2E * HEAD_DIM ** -0.5)).astype(q_ref.dtype)
    k_ref[0] = head_norm(k, kn_ref[...]).astype(k_ref.dtype)
    v_ref[0] = v.astype(v_ref.dtype)


def _qkv(x, mod, mix_norm, w_qkv, q_norm, k_norm):
    bsz, seq, d = x.shape
    rows = QKV_ROWS
    assert seq % rows == 0
    n_heads = d // HEAD_DIM
    head_of_lane = jnp.arange(d) // HEAD_DIM
    pool = (head_of_lane[:, None] == head_of_lane[None, :]).astype(jnp.bfloat16) * (1.0 / HEAD_DIM)
    tile = pl.BlockSpec((1, rows, d), lambda b, s: (b, s, 0))
    out = jax.ShapeDtypeStruct((bsz, seq, d), jnp.bfloat16)
    return pl.pallas_call(
        _qkv_kernel,
        grid=(bsz, seq // rows),
        in_specs=[
            tile,
            pl.BlockSpec((1, N_MOD, d), lambda b, s: (b, 0, 0)),
            _const_spec((1, d)),
            _const_spec((d, 3 * d)),
            _const_spec((d, d)),
            _const_spec((1, d)),
            _const_spec((1, d)),
        ],
        out_specs=[tile, tile, tile],
        out_shape=[out, out, out],
        compiler_params=pltpu.CompilerParams(
            dimension_semantics=("arbitrary", "arbitrary"),
            vmem_limit_bytes=V7X_VMEM_LIMIT_BYTES),
        name="l1_qkv",
    )(x, mod, mix_norm.reshape(1, d), w_qkv.astype(jnp.bfloat16), pool.astype(jnp.bfloat16),
      jnp.tile(q_norm, n_heads).reshape(1, d), jnp.tile(k_norm, n_heads).reshape(1, d))


MASK_BIAS = -1e30
SKIP_LOG2 = 150.0
ITEM_PAD = 2


def _attention_kernel(q_ref, k_ref, v_ref, later_ref, bias_ref, o_ref,
                      q_heads, acc, cs, sp16, m32, w16, rs, cmin, items, *, n_blocks):
    blk = ATT_BLOCK
    n_heads = HEADS_PER_LANE_GROUP
    seq = n_blocks * blk
    trash = n_blocks

    q = q_ref[0]
    lane = lax.broadcasted_iota(jnp.int32, q.shape, 1)
    for h in range(n_heads):
        in_head = (lane >= h * HEAD_DIM) & (lane < (h + 1) * HEAD_DIM)
        q_heads[h, 0:seq, :] = jnp.where(in_head, q, jnp.zeros_like(q))
        q_heads[h, seq:seq + blk, :] = jnp.zeros((blk, V7X_LANES), q_heads.dtype)
        acc[h, seq:seq + blk, :] = jnp.zeros((blk, V7X_LANES), jnp.float32)
        cs[h, seq:seq + blk, :] = jnp.zeros((blk, 1), jnp.float32)

    def item(n, dist):
        qi = items[n]
        kj = jnp.minimum(qi, n_blocks - 1) - dist
        return pl.multiple_of(qi * blk, blk), pl.multiple_of(kj * blk, blk), qi

    def stage1(n, slot, dist, diag):
        q_row, k_row, _ = item(n, dist)
        k = k_ref[0, pl.ds(k_row, blk), :]
        for h in range(n_heads):
            z = lax.dot_general(q_heads[h, pl.ds(q_row, blk), :], k, (((1,), (1,)), ((), ())),
                                preferred_element_type=jnp.float32)
            if diag:
                z = z + bias_ref[...]
            sp = jnp.maximum(z, 0.0) + jnp.log(1.0 + jnp.exp2(-jnp.abs(z))) * LOG2E
            m32[slot, h] = z - sp
            sp16[slot, h] = sp.astype(jnp.bfloat16)
            rs[slot, h] = jnp.sum(sp, axis=1, keepdims=True)

    def stage2(n, slot, dist, diag):
        q_row, _, qi = item(n, dist)
        lowest = None
        for h in range(n_heads):
            t = m32[slot, h] - _dot(sp16[slot, h], later_ref[...])
            c = rs[slot, h]
            if not diag:
                prev = cs[h, pl.ds(q_row, blk), :]
                t = t - prev
                c = c + prev
            w16[slot, h] = jnp.exp2(t).astype(jnp.bfloat16)
            cs[h, pl.ds(q_row, blk), :] = c
            low = jnp.min(c, axis=0, keepdims=True)
            lowest = low if lowest is None else jnp.minimum(lowest, low)
        cmin[pl.ds(qi, 1), :] = jnp.broadcast_to(lowest, (1, V7X_LANES))

    def stage3(n, slot, dist, diag):
        q_row, k_row, _ = item(n, dist)
        v = v_ref[0, pl.ds(k_row, blk), :]
        for h in range(n_heads):
            pv = _dot(w16[slot, h], v)
            if diag:
                acc[h, pl.ds(q_row, blk), :] = pv
            else:
                acc[h, pl.ds(q_row, blk), :] += pv

    def run_wave(n_items, kinds):
        def step(it, slot, stages):
            if 1 in stages:
                stage1(it, slot, *kinds[slot])
            if 2 in stages:
                stage2(it - 1, 1 - slot, *kinds[1 - slot])
            if 3 in stages:
                stage3(it - 2, slot, *kinds[slot])

        step(0, 0, (1,))
        step(1, 1, (1, 2))

        def pair(t, carry):
            it = 2 * t + 2
            step(it, 0, (1, 2, 3))
            step(it + 1, 1, (1, 2, 3))
            return carry

        lax.fori_loop(0, (n_items - 2) // 2, pair, 0)
        step(n_items, 0, (2, 3))
        step(n_items + 1, 1, (3,))

    def list_active(dist):
        rows = cmin[0:n_blocks, :]
        block = lax.broadcasted_iota(jnp.int32, rows.shape, 0)
        active = (rows < SKIP_LOG2) & (block >= dist)
        weights = jnp.where(active, jnp.exp2(block.astype(jnp.float32)), 0.0)
        mask = (jnp.sum(weights) * (1.0 / V7X_LANES)).astype(jnp.int32)
        count = jnp.int32(0)
        for i in range(n_blocks):
            items[count] = i
            count = count + ((mask >> i) & 1)
        for p in range(ITEM_PAD):
            items[count + p] = trash
        return count

    for i in range(n_blocks):
        items[2 * i] = i
        items[2 * i + 1] = i if i > 0 else trash
    run_wave(2 * n_blocks, ((0, True), (1, False)))

    def more_waves(state):
        dist, count = state
        run_wave(count + (count & 1), ((dist, False), (dist, False)))
        return dist + 1, list_active(dist + 1)

    lax.while_loop(lambda state: state[1] > 0, more_waves, (jnp.int32(2), list_active(2)))

    out = acc[0, 0:seq, :]
    for h in range(1, n_heads):
        out = jnp.where(lane >= h * HEAD_DIM, acc[h, 0:seq, :], out)
    o_ref[0] = out.astype(o_ref.dtype)


def _attention(q, k, v):
    bsz, seq, d = q.shape
    blk = ATT_BLOCK
    n_heads = HEADS_PER_LANE_GROUP
    assert seq % blk == 0 and d % V7X_LANES == 0
    n_blocks = seq // blk
    rows = seq + blk
    idx = jnp.arange(blk)
    later = (idx[:, None] > idx[None, :]).astype(jnp.bfloat16)
    bias = jnp.where(idx[None, :] < idx[:, None], 0.0, MASK_BIAS).astype(jnp.float32)
    seq_spec = pl.BlockSpec((1, seq, V7X_LANES), lambda b, p: (b, 0, p))
    cmin_rows = pl.cdiv(n_blocks + 1, V7X_SUBLANES) * V7X_SUBLANES
    return pl.pallas_call(
        functools.partial(_attention_kernel, n_blocks=n_blocks),
        grid=(bsz, d // V7X_LANES),
        in_specs=[seq_spec, seq_spec, seq_spec, _const_spec((blk, blk)), _const_spec((blk, blk))],
        out_specs=seq_spec,
        out_shape=jax.ShapeDtypeStruct((bsz, seq, d), jnp.bfloat16),
        scratch_shapes=[
            pltpu.VMEM((n_heads, rows, V7X_LANES), jnp.bfloat16),
            pltpu.VMEM((n_heads, rows, V7X_LANES), jnp.float32),
            pltpu.VMEM((n_heads, rows, 1), jnp.float32),
            pltpu.VMEM((2, n_heads, blk, blk), jnp.bfloat16),
            pltpu.VMEM((2, n_heads, blk, blk), jnp.float32),
            pltpu.VMEM((2, n_heads, blk, blk), jnp.bfloat16),
            pltpu.VMEM((2, n_heads, blk, 1), jnp.float32),
            pltpu.VMEM((cmin_rows, V7X_LANES), jnp.float32),
            pltpu.SMEM((2 * n_blocks + ITEM_PAD,), jnp.int32),
        ],
        compiler_params=pltpu.CompilerParams(
            dimension_semantics=("arbitrary", "arbitrary"),
            vmem_limit_bytes=V7X_VMEM_LIMIT_BYTES),
        name="sb_attention",
    )(q, k, v, later, bias)


def kernel(x, c, l0_ada_w, l0_ada_b, l0_mix_norm, l0_w_in, l0_conv_a_w, l0_conv_b_w, l0_conv_b_b, l0_rg_a_w, l0_rg_a_b, l0_rg_x_w, l0_rg_x_b, l0_rg_lambda, l0_w_out, l0_ffn_norm, l0_ffn_w_gate, l0_ffn_w_up, l0_ffn_w_down, l1_ada_w, l1_ada_b, l1_mix_norm, l1_w_qkv, l1_q_norm, l1_k_norm, l1_w_out, l1_ffn_norm, l1_ffn_w_gate, l1_ffn_w_up, l1_ffn_w_down):
    mod0 = _ada_modulation(c, l0_ada_w, l0_ada_b)
    mod1 = _ada_modulation(c, l1_ada_w, l1_ada_b)

    y = _mixer(x, mod0, l0_mix_norm, l0_w_in, l0_conv_a_w, l0_conv_b_w, l0_conv_b_b,
               l0_rg_a_w, l0_rg_a_b, l0_rg_x_w, l0_rg_x_b, l0_rg_lambda)
    x = _proj_ffn(x, y, mod0, l0_w_out, l0_ffn_norm, l0_ffn_w_gate, l0_ffn_w_up, l0_ffn_w_down)

    q, k, v = _qkv(x, mod1, l1_mix_norm, l1_w_qkv, l1_q_norm, l1_k_norm)
    o = _attention(q, k, v)
    x = _proj_ffn(x, o, mod1, l1_w_out, l1_ffn_norm, l1_ffn_w_gate, l1_ffn_w_up, l1_ffn_w_down)
    return x
```

```python
import functools
import math

import jax
import jax.numpy as jnp
from jax import lax
from jax.experimental import pallas as pl
from jax.experimental.pallas import tpu as pltpu

D_MODEL = 1024
HEAD_DIM = 64
CONV_WIDTH = D_MODEL // 2
LRU_WIDTH = D_MODEL // 2
SHORT_CONV_K = 3
LRU_CONV_K = 4
RG_C = 8.0
N_MOD = 6
EPS = 1e-6
LOG2E = math.log2(math.e)

V7X_LANES = 128
V7X_SUBLANES = 8
V7X_MXU_DIM = 256
V7X_VMEM_LIMIT_BYTES = 56 * 1024 * 1024

HEADS_PER_LANE_GROUP = V7X_LANES // HEAD_DIM

MIX_ROWS = 256
FFN_ROWS = 512
QKV_ROWS = 512
ATT_BLOCK = 256
FFN_CHUNK = V7X_MXU_DIM
ADA_COLS = 1536


def _const_spec(shape):
    zeros = (0,) * len(shape)
    return pl.BlockSpec(shape, lambda *_: zeros, pipeline_mode=pl.Buffered(1))


def _sigmoid(x):
    return 1.0 / (1.0 + jnp.exp(-x))


def _softplus(x):
    return jnp.maximum(x, 0.0) + jnp.log(1.0 + jnp.exp(-jnp.abs(x)))


def _gelu_tanh(x):
    c = math.sqrt(2.0 / math.pi)
    return 0.5 * x * (1.0 + jnp.tanh(c * (x + 0.044715 * (x * x * x))))


def _rms_norm_modulate(x, g, shift, scale):
    ms = jnp.mean(x * x, axis=-1, keepdims=True)
    y = x * lax.rsqrt(ms + EPS) * g
    return y * (1.0 + scale) + shift


def _split_bf16(x):
    hi = x.astype(jnp.bfloat16)
    lo = (x - hi.astype(jnp.float32)).astype(jnp.bfloat16)
    return hi, lo


def _dot(a, b):
    return jnp.dot(a, b, preferred_element_type=jnp.float32)


def _ada_kernel(c_ref, w_ref, b_ref, o_ref):
    c = c_ref[...]
    s = c * _sigmoid(c)
    s_hi, s_lo = _split_bf16(s)
    w_hi, w_lo = _split_bf16(w_ref[...])
    acc = _dot(s_hi, w_hi) + _dot(s_hi, w_lo) + _dot(s_lo, w_hi)
    o_ref[...] = acc + b_ref[...]


def _ada_modulation(c, ada_w, ada_b):
    bsz, d = c.shape
    n = ada_w.shape[1]
    assert n % ADA_COLS == 0
    out = pl.pallas_call(
        _ada_kernel,
        grid=(n // ADA_COLS,),
        in_specs=[
            pl.BlockSpec((bsz, d), lambda j: (0, 0)),
            pl.BlockSpec((d, ADA_COLS), lambda j: (0, j)),
            pl.BlockSpec((1, ADA_COLS), lambda j: (0, j)),
        ],
        out_specs=pl.BlockSpec((bsz, ADA_COLS), lambda j: (0, j)),
        out_shape=jax.ShapeDtypeStruct((bsz, n), jnp.float32),
        compiler_params=pltpu.CompilerParams(dimension_semantics=("arbitrary",)),
        name="ada_modulation",
    )(c, ada_w, ada_b.reshape(1, n))
    return out.reshape(bsz, N_MOD, d)


def _linear_scan(a, b, h0):
    n, width = a.shape
    group = V7X_SUBLANES
    row = lax.broadcasted_iota(jnp.int32, (group, width), 0)
    h = h0
    out = []
    for g in range(n // group):
        ag = a[g * group:(g + 1) * group, :]
        bg = b[g * group:(g + 1) * group, :]
        d = 1
        while d < group:
            a_prev = jnp.where(row >= d, pltpu.roll(ag, d, 0), 1.0)
            b_prev = jnp.where(row >= d, pltpu.roll(bg, d, 0), 0.0)
            bg = ag * b_prev + bg
            ag = ag * a_prev
            d *= 2
        hg = ag * h + bg
        h = hg[group - 1:group, :]
        out.append(hg)
    return jnp.concatenate(out, axis=0)


def _mixer_kernel(x_ref, mod_ref, norm_ref, w_in_ref, conv_a_ref, conv_b_ref, conv_bb_ref,
                  wa_ref, ba_ref, wx_ref, bx_ref, lam_ref, y_ref,
                  p_hist, r_hist, h_state):
    rows = x_ref.shape[1]
    hist = V7X_SUBLANES
    first = pl.program_id(1) == 0

    @pl.when(first)
    def _():
        p_hist[0:hist, :] = jnp.zeros((hist, CONV_WIDTH), jnp.float32)
        r_hist[0:hist, :] = jnp.zeros((hist, LRU_WIDTH), jnp.float32)
        h_state[...] = jnp.zeros_like(h_state)

    @pl.when(jnp.logical_not(first))
    def _():
        p_hist[0:hist, :] = p_hist[rows:rows + hist, :]
        r_hist[0:hist, :] = r_hist[rows:rows + hist, :]

    x = x_ref[0]
    h = _rms_norm_modulate(x, norm_ref[...], mod_ref[0, 0:1, :], mod_ref[0, 1:2, :])
    u = _dot(h.astype(jnp.bfloat16), w_in_ref[...])
    w = CONV_WIDTH
    a_b, a_c, a_x = u[:, 0:w], u[:, w:2 * w], u[:, 2 * w:3 * w]
    r_gate, r_x = u[:, 3 * w:4 * w], u[:, 4 * w:5 * w]

    p = a_c * a_x
    p_hist[hist:hist + rows, :] = p
    conv = conv_a_ref[SHORT_CONV_K - 1:SHORT_CONV_K, :] * p
    for k in range(SHORT_CONV_K - 1):
        back = SHORT_CONV_K - 1 - k
        conv = conv + conv_a_ref[k:k + 1, :] * p_hist[pl.ds(hist - back, rows), :]
    y_ref[0, :, 0:w] = (a_b * conv).astype(y_ref.dtype)

    r_hist[hist:hist + rows, :] = r_x
    xr = conv_b_ref[LRU_CONV_K - 1:LRU_CONV_K, :] * r_x + conv_bb_ref[...]
    for k in range(LRU_CONV_K - 1):
        back = LRU_CONV_K - 1 - k
        xr = xr + conv_b_ref[k:k + 1, :] * r_hist[pl.ds(hist - back, rows), :]
    xr16 = xr.astype(jnp.bfloat16)
    r = _sigmoid(_dot(xr16, wa_ref[...]) + ba_ref[...])
    i = _sigmoid(_dot(xr16, wx_ref[...]) + bx_ref[...])
    log_a = (-RG_C) * r * _softplus(-lam_ref[...])
    a = jnp.exp(log_a)
    b = jnp.sqrt(1.0 - a * a) * (i * xr)
    hs = _linear_scan(a, b, h_state[0:1, :])
    h_state[0:1, :] = hs[rows - 1:rows, :]
    y_ref[0, :, w:2 * w] = (_gelu_tanh(r_gate) * hs).astype(y_ref.dtype)


def _block_diag(w):
    n_blk, blk, _ = w.shape
    eye = jnp.eye(n_blk, dtype=w.dtype)
    return jnp.einsum('hij,hg->higj', w, eye).reshape(n_blk * blk, n_blk * blk)


def _mixer(x, mod, mix_norm, w_in, conv_a_w, conv_b_w, conv_b_b,
           rg_a_w, rg_a_b, rg_x_w, rg_x_b, rg_lambda):
    bsz, seq, d = x.shape
    rows = MIX_ROWS
    assert seq % rows == 0
    n_in = w_in.shape[1]
    w = LRU_WIDTH
    return pl.pallas_call(
        _mixer_kernel,
        grid=(bsz, seq // rows),
        in_specs=[
            pl.BlockSpec((1, rows, d), lambda b, s: (b, s, 0)),
            pl.BlockSpec((1, N_MOD, d), lambda b, s: (b, 0, 0)),
            _const_spec((1, d)),
            _const_spec((d, n_in)),
            _const_spec((SHORT_CONV_K, CONV_WIDTH)),
            _const_spec((LRU_CONV_K, w)),
            _const_spec((1, w)),
            _const_spec((w, w)),
            _const_spec((1, w)),
            _const_spec((w, w)),
            _const_spec((1, w)),
            _const_spec((1, w)),
        ],
        out_specs=pl.BlockSpec((1, rows, d), lambda b, s: (b, s, 0)),
        out_shape=jax.ShapeDtypeStruct((bsz, seq, d), jnp.bfloat16),
        scratch_shapes=[
            pltpu.VMEM((V7X_SUBLANES + rows, CONV_WIDTH), jnp.float32),
            pltpu.VMEM((V7X_SUBLANES + rows, w), jnp.float32),
            pltpu.VMEM((V7X_SUBLANES, w), jnp.float32),
        ],
        compiler_params=pltpu.CompilerParams(
            dimension_semantics=("arbitrary", "arbitrary"),
            vmem_limit_bytes=V7X_VMEM_LIMIT_BYTES),
        name="l0_mixer",
    )(x, mod, mix_norm.reshape(1, d), w_in.astype(jnp.bfloat16), conv_a_w, conv_b_w,
      conv_b_b.reshape(1, w),
      _block_diag(rg_a_w).astype(jnp.bfloat16), rg_a_b.reshape(1, w),
      _block_diag(rg_x_w).astype(jnp.bfloat16), rg_x_b.reshape(1, w),
      rg_lambda.reshape(1, w))


def _proj_ffn_kernel(x_ref, y_ref, mod_ref, w_out_ref, norm_ref, wg_ref, wu_ref, wd_ref,
                     o_ref, acc_ref):
    x = x_ref[0]
    x1 = x + mod_ref[0, 2:3, :] * _dot(y_ref[0], w_out_ref[...])
    hf = _rms_norm_modulate(x1, norm_ref[...], mod_ref[0, 3:4, :], mod_ref[0, 4:5, :])
    hf16 = hf.astype(jnp.bfloat16)
    for c in range(wg_ref.shape[1] // FFN_CHUNK):
        cols = slice(c * FFN_CHUNK, (c + 1) * FFN_CHUNK)
        g = _dot(hf16, wg_ref[:, cols])
        up = _dot(hf16, wu_ref[:, cols])
        act = (g * _sigmoid(g) * up).astype(jnp.bfloat16)
        part = _dot(act, wd_ref[cols, :])
        if c == 0:
            acc_ref[...] = part
        else:
            acc_ref[...] += part
    o_ref[0] = x1 + mod_ref[0, 5:6, :] * acc_ref[...]


def _proj_ffn(x, y, mod, w_out, ffn_norm, w_gate, w_up, w_down):
    bsz, seq, d = x.shape
    rows = FFN_ROWS
    d_ff = w_gate.shape[1]
    assert seq % rows == 0 and d_ff % FFN_CHUNK == 0
    return pl.pallas_call(
        _proj_ffn_kernel,
        grid=(bsz, seq // rows),
        in_specs=[
            pl.BlockSpec((1, rows, d), lambda b, s: (b, s, 0)),
            pl.BlockSpec((1, rows, d), lambda b, s: (b, s, 0)),
            pl.BlockSpec((1, N_MOD, d), lambda b, s: (b, 0, 0)),
            _const_spec((d, d)),
            _const_spec((1, d)),
            _const_spec((d, d_ff)),
            _const_spec((d, d_ff)),
            _const_spec((d_ff, d)),
        ],
        out_specs=pl.BlockSpec((1, rows, d), lambda b, s: (b, s, 0)),
        out_shape=jax.ShapeDtypeStruct((bsz, seq, d), jnp.float32),
        scratch_shapes=[pltpu.VMEM((rows, d), jnp.float32)],
        compiler_params=pltpu.CompilerParams(
            dimension_semantics=("arbitrary", "arbitrary"),
            vmem_limit_bytes=V7X_VMEM_LIMIT_BYTES),
        name="proj_ffn",
    )(x, y, mod, w_out.astype(jnp.bfloat16), ffn_norm.reshape(1, d),
      w_gate.astype(jnp.bfloat16), w_up.astype(jnp.bfloat16), w_down.astype(jnp.bfloat16))


def _qkv_kernel(x_ref, mod_ref, norm_ref, w_ref, pool_ref, qn_ref, kn_ref,
                q_ref, k_ref, v_ref):
    d = x_ref.shape[2]
    x = x_ref[0]
    h = _rms_norm_modulate(x, norm_ref[...], mod_ref[0, 0:1, :], mod_ref[0, 1:2, :])
    qkv = _dot(h.astype(jnp.bfloat16), w_ref[...])
    q, k, v = qkv[:, 0:d], qkv[:, d:2 * d], qkv[:, 2 * d:3 * d]

    def head_norm(t, g):
        ms = _dot((t * t).astype(jnp.bfloat16), pool_ref[...])
        return t * lax.rsqrt(ms + EPS) * g

    q_ref[0] = (head_norm(q, qn_ref[...]) * (LOG2E * HEAD_DIM ** -0.5)).astype(q_ref.dtype)
    k_ref[0] = head_norm(k, kn_ref[...]).astype(k_ref.dtype)
    v_ref[0] = v.astype(v_ref.dtype)


def _qkv(x, mod, mix_norm, w_qkv, q_norm, k_norm):
    bsz, seq, d = x.shape
    rows = QKV_ROWS
    assert seq % rows == 0
    n_heads = d // HEAD_DIM
    head_of_lane = jnp.arange(d) // HEAD_DIM
    pool = (head_of_lane[:, None] == head_of_lane[None, :]).astype(jnp.bfloat16) * (1.0 / HEAD_DIM)
    tile = pl.BlockSpec((1, rows, d), lambda b, s: (b, s, 0))
    out = jax.ShapeDtypeStruct((bsz, seq, d), jnp.bfloat16)
    return pl.pallas_call(
        _qkv_kernel,
        grid=(bsz, seq // rows),
        in_specs=[
            tile,
            pl.BlockSpec((1, N_MOD, d), lambda b, s: (b, 0, 0)),
            _const_spec((1, d)),
            _const_spec((d, 3 * d)),
            _const_spec((d, d)),
            _const_spec((1, d)),
            _const_spec((1, d)),
        ],
        out_specs=[tile, tile, tile],
        out_shape=[out, out, out],
        compiler_params=pltpu.CompilerParams(
            dimension_semantics=("arbitrary", "arbitrary"),
            vmem_limit_bytes=V7X_VMEM_LIMIT_BYTES),
        name="l1_qkv",
    )(x, mod, mix_norm.reshape(1, d), w_qkv.astype(jnp.bfloat16), pool.astype(jnp.bfloat16),
      jnp.tile(q_norm, n_heads).reshape(1, d), jnp.tile(k_norm, n_heads).reshape(1, d))


MASK_BIAS = -1e30
SKIP_LOG2 = 150.0
ITEM_PAD = 2


def _attention_kernel(q_ref, k_ref, v_ref, later_ref, bias_ref, o_ref,
                      q_heads, acc, cs, sp16, m32, w16, rs, cmin, items, *, n_blocks):
    blk = ATT_BLOCK
    n_heads = HEADS_PER_LANE_GROUP
    rows = n_heads * blk
    trash = n_blocks

    lane = lax.broadcasted_iota(jnp.int32, (blk, V7X_LANES), 1)
    for i in range(n_blocks):
        q = q_ref[0, i * blk:(i + 1) * blk, :]
        for h in range(n_heads):
            in_head = (lane >= h * HEAD_DIM) & (lane < (h + 1) * HEAD_DIM)
            q_heads[i, h * blk:(h + 1) * blk, :] = jnp.where(in_head, q, jnp.zeros_like(q))
    q_heads[trash] = jnp.zeros((rows, V7X_LANES), q_heads.dtype)
    acc[trash] = jnp.zeros((rows, V7X_LANES), jnp.float32)
    cs[trash] = jnp.zeros((rows, 1), jnp.float32)

    def item(n, dist):
        qi = items[n]
        kj = jnp.minimum(qi, n_blocks - 1) - dist
        return qi, pl.multiple_of(kj * blk, blk)

    def stage1(n, slot, dist, diag):
        qi, k_row = item(n, dist)
        z = lax.dot_general(q_heads[qi], k_ref[0, pl.ds(k_row, blk), :], (((1,), (1,)), ((), ())),
                            preferred_element_type=jnp.float32)
        if diag:
            z = z + bias_ref[...]
        sp = jnp.maximum(z, 0.0) + jnp.log(1.0 + jnp.exp2(-jnp.abs(z))) * LOG2E
        m32[slot] = z - sp
        sp16[slot] = sp.astype(jnp.bfloat16)
        rs[slot] = jnp.sum(sp, axis=1, keepdims=True)

    def stage2(n, slot, dist, diag):
        qi, _ = item(n, dist)
        t = m32[slot] - _dot(sp16[slot], later_ref[...])
        c = rs[slot]
        if not diag:
            prev = cs[qi]
            t = t - prev
            c = c + prev
        w16[slot] = jnp.exp2(t).astype(jnp.bfloat16)
        cs[qi] = c
        cmin[pl.ds(qi, 1), :] = jnp.broadcast_to(jnp.min(c, axis=0, keepdims=True), (1, V7X_LANES))

    def stage3(n, slot, dist, diag):
        qi, k_row = item(n, dist)
        pv = _dot(w16[slot], v_ref[0, pl.ds(k_row, blk), :])
        if diag:
            acc[qi] = pv
        else:
            acc[qi] += pv

    def run_wave(n_items, kinds):
        def step(it, slot, stages):
            if 1 in stages:
                stage1(it, slot, *kinds[slot])
            if 2 in stages:
                stage2(it - 1, 1 - slot, *kinds[1 - slot])
            if 3 in stages:
                stage3(it - 2, slot, *kinds[slot])

        step(0, 0, (1,))
        step(1, 1, (1, 2))

        def pair(t, carry):
            it = 2 * t + 2
            step(it, 0, (1, 2, 3))
            step(it + 1, 1, (1, 2, 3))
            return carry

        lax.fori_loop(0, (n_items - 2) // 2, pair, 0)
        step(n_items, 0, (2, 3))
        step(n_items + 1, 1, (3,))

    def list_active(dist):
        low = cmin[0:n_blocks, :]
        block = lax.broadcasted_iota(jnp.int32, low.shape, 0)
        active = (low < SKIP_LOG2) & (block >= dist)
        weights = jnp.where(active, jnp.left_shift(1, block), 0).astype(jnp.float32)
        mask = (jnp.sum(weights) * (1.0 / V7X_LANES)).astype(jnp.int32)
        count = jnp.int32(0)
        for i in range(n_blocks):
            items[count] = i
            count = count + ((mask >> i) & 1)
        for p in range(ITEM_PAD):
            items[count + p] = trash
        return count

    for i in range(n_blocks):
        items[2 * i] = i
        items[2 * i + 1] = i if i > 0 else trash
    run_wave(2 * n_blocks, ((0, True), (1, False)))

    def more_waves(state):
        dist, count = state
        run_wave(count + (count & 1), ((dist, False), (dist, False)))
        return dist + 1, list_active(dist + 1)

    lax.while_loop(lambda state: state[1] > 0, more_waves, (jnp.int32(2), list_active(2)))

    for i in range(n_blocks):
        out = acc[i, 0:blk, :]
        for h in range(1, n_heads):
            out = jnp.where(lane >= h * HEAD_DIM, acc[i, h * blk:(h + 1) * blk, :], out)
        o_ref[0, i * blk:(i + 1) * blk, :] = out.astype(o_ref.dtype)


def _attention(q, k, v):
    bsz, seq, d = q.shape
    blk = ATT_BLOCK
    n_heads = HEADS_PER_LANE_GROUP
    assert seq % blk == 0 and d % V7X_LANES == 0
    n_blocks = seq // blk
    rows = n_heads * blk
    idx = jnp.arange(blk)
    later = (idx[:, None] > idx[None, :]).astype(jnp.bfloat16)
    bias = jnp.where(idx[None, :] < idx[:, None], 0.0, MASK_BIAS).astype(jnp.float32)
    bias = jnp.tile(bias, (n_heads, 1))
    seq_spec = pl.BlockSpec((1, seq, V7X_LANES), lambda b, p: (b, 0, p))
    cmin_rows = pl.cdiv(n_blocks + 1, V7X_SUBLANES) * V7X_SUBLANES
    return pl.pallas_call(
        functools.partial(_attention_kernel, n_blocks=n_blocks),
        grid=(bsz, d // V7X_LANES),
        in_specs=[seq_spec, seq_spec, seq_spec, _const_spec((blk, blk)), _const_spec((rows, blk))],
        out_specs=seq_spec,
        out_shape=jax.ShapeDtypeStruct((bsz, seq, d), jnp.bfloat16),
        scratch_shapes=[
            pltpu.VMEM((n_blocks + 1, rows, V7X_LANES), jnp.bfloat16),
            pltpu.VMEM((n_blocks + 1, rows, V7X_LANES), jnp.float32),
            pltpu.VMEM((n_blocks + 1, rows, 1), jnp.float32),
            pltpu.VMEM((2, rows, blk), jnp.bfloat16),
            pltpu.VMEM((2, rows, blk), jnp.float32),
            pltpu.VMEM((2, rows, blk), jnp.bfloat16),
            pltpu.VMEM((2, rows, 1), jnp.float32),
            pltpu.VMEM((cmin_rows, V7X_LANES), jnp.float32),
            pltpu.SMEM((2 * n_blocks + ITEM_PAD,), jnp.int32),
        ],
        compiler_params=pltpu.CompilerParams(
            dimension_semantics=("arbitrary", "arbitrary"),
            vmem_limit_bytes=V7X_VMEM_LIMIT_BYTES),
        name="sb_attention",
    )(q, k, v, later, bias)


def kernel(x, c, l0_ada_w, l0_ada_b, l0_mix_norm, l0_w_in, l0_conv_a_w, l0_conv_b_w, l0_conv_b_b, l0_rg_a_w, l0_rg_a_b, l0_rg_x_w, l0_rg_x_b, l0_rg_lambda, l0_w_out, l0_ffn_norm, l0_ffn_w_gate, l0_ffn_w_up, l0_ffn_w_down, l1_ada_w, l1_ada_b, l1_mix_norm, l1_w_qkv, l1_q_norm, l1_k_norm, l1_w_out, l1_ffn_norm, l1_ffn_w_gate, l1_ffn_w_up, l1_ffn_w_down):
    mod0 = _ada_modulation(c, l0_ada_w, l0_ada_b)
    mod1 = _ada_modulation(c, l1_ada_w, l1_ada_b)

    y = _mixer(x, mod0, l0_mix_norm, l0_w_in, l0_conv_a_w, l0_conv_b_w, l0_conv_b_b,
               l0_rg_a_w, l0_rg_a_b, l0_rg_x_w, l0_rg_x_b, l0_rg_lambda)
    x = _proj_ffn(x, y, mod0, l0_w_out, l0_ffn_norm, l0_ffn_w_gate, l0_ffn_w_up, l0_ffn_w_down)

    q, k, v = _qkv(x, mod1, l1_mix_norm, l1_w_qkv, l1_q_norm, l1_k_norm)
    o = _attention(q, k, v)
    x = _proj_ffn(x, o, mod1, l1_w_out, l1_ffn_norm, l1_ffn_w_gate, l1_ffn_w_up, l1_ffn_w_down)
    return x
```

```python
import functools
import math

import jax
import jax.numpy as jnp
from jax import lax
from jax.experimental import pallas as pl
from jax.experimental.pallas import tpu as pltpu

D_MODEL = 1024
HEAD_DIM = 64
CONV_WIDTH = D_MODEL // 2
LRU_WIDTH = D_MODEL // 2
SHORT_CONV_K = 3
LRU_CONV_K = 4
RG_C = 8.0
N_MOD = 6
EPS = 1e-6
LOG2E = math.log2(math.e)

V7X_LANES = 128
V7X_SUBLANES = 8
V7X_MXU_DIM = 256
V7X_VMEM_LIMIT_BYTES = 56 * 1024 * 1024

HEADS_PER_LANE_GROUP = V7X_LANES // HEAD_DIM

MIX_ROWS = 512
FFN_ROWS = 512
QKV_ROWS = 512
ATT_BLOCK = 256
FFN_CHUNK = V7X_MXU_DIM
ADA_COLS = 1536


def _const_spec(shape):
    zeros = (0,) * len(shape)
    return pl.BlockSpec(shape, lambda *_: zeros, pipeline_mode=pl.Buffered(1))


def _sigmoid(x):
    return 1.0 / (1.0 + jnp.exp(-x))


def _softplus(x):
    return jnp.maximum(x, 0.0) + jnp.log(1.0 + jnp.exp(-jnp.abs(x)))


def _gelu_tanh(x):
    c = math.sqrt(2.0 / math.pi)
    return 0.5 * x * (1.0 + jnp.tanh(c * (x + 0.044715 * (x * x * x))))


def _rms_norm_modulate(x, g, shift, scale):
    ms = jnp.mean(x * x, axis=-1, keepdims=True)
    return x * lax.rsqrt(ms + EPS) * (g * (1.0 + scale)) + shift


def _split_bf16(x):
    hi = x.astype(jnp.bfloat16)
    lo = (x - hi.astype(jnp.float32)).astype(jnp.bfloat16)
    return hi, lo


def _dot(a, b):
    return jnp.dot(a, b, preferred_element_type=jnp.float32)


def _ada_kernel(c_ref, w_ref, b_ref, o_ref):
    c = c_ref[...]
    s = c * _sigmoid(c)
    s_hi, s_lo = _split_bf16(s)
    w_hi, w_lo = _split_bf16(w_ref[...])
    acc = _dot(s_hi, w_hi) + _dot(s_hi, w_lo) + _dot(s_lo, w_hi)
    o_ref[...] = acc + b_ref[...]


def _ada_modulation(c, ada_w, ada_b):
    bsz, d = c.shape
    n = ada_w.shape[1]
    assert n % ADA_COLS == 0
    out = pl.pallas_call(
        _ada_kernel,
        grid=(n // ADA_COLS,),
        in_specs=[
            pl.BlockSpec((bsz, d), lambda j: (0, 0)),
            pl.BlockSpec((d, ADA_COLS), lambda j: (0, j)),
            pl.BlockSpec((1, ADA_COLS), lambda j: (0, j)),
        ],
        out_specs=pl.BlockSpec((bsz, ADA_COLS), lambda j: (0, j)),
        out_shape=jax.ShapeDtypeStruct((bsz, n), jnp.float32),
        compiler_params=pltpu.CompilerParams(dimension_semantics=("arbitrary",)),
        name="ada_modulation",
    )(c, ada_w, ada_b.reshape(1, n))
    return out.reshape(bsz, N_MOD, d)


def _linear_scan(a, b, h0):
    n, width = a.shape
    group = V7X_SUBLANES
    row = lax.broadcasted_iota(jnp.int32, (group, width), 0)
    h = h0
    out = []
    for g in range(n // group):
        ag = a[g * group:(g + 1) * group, :]
        bg = b[g * group:(g + 1) * group, :]
        d = 1
        while d < group:
            a_prev = jnp.where(row >= d, pltpu.roll(ag, d, 0), 1.0)
            b_prev = jnp.where(row >= d, pltpu.roll(bg, d, 0), 0.0)
            bg = ag * b_prev + bg
            ag = ag * a_prev
            d *= 2
        hg = ag * h + bg
        h = hg[group - 1:group, :]
        out.append(hg)
    return jnp.concatenate(out, axis=0)


def _mixer_kernel(x_ref, mod_ref, norm_ref, w_in_ref, conv_a_ref, conv_b_ref, conv_bb_ref,
                  wa_ref, ba_ref, wx_ref, bx_ref, lam_ref, y_ref,
                  p_hist, r_hist, h_state):
    rows = x_ref.shape[1]
    hist = V7X_SUBLANES
    first = pl.program_id(1) == 0

    @pl.when(first)
    def _():
        p_hist[...] = jnp.zeros_like(p_hist)
        r_hist[...] = jnp.zeros_like(r_hist)
        h_state[...] = jnp.zeros_like(h_state)

    x = x_ref[0]
    h = _rms_norm_modulate(x, norm_ref[...], mod_ref[0, 0:1, :], mod_ref[0, 1:2, :])
    u = _dot(h.astype(jnp.bfloat16), w_in_ref[...])
    w = CONV_WIDTH
    a_b, a_c, a_x = u[:, 0:w], u[:, w:2 * w], u[:, 2 * w:3 * w]
    r_gate, r_x = u[:, 3 * w:4 * w], u[:, 4 * w:5 * w]

    def causal_conv(v, hist_ref, taps_ref, n_taps):
        ext = jnp.concatenate([hist_ref[...], v], axis=0)
        hist_ref[...] = v[rows - hist:rows, :]
        out = taps_ref[n_taps - 1:n_taps, :] * v
        for k in range(n_taps - 1):
            back = n_taps - 1 - k
            out = out + taps_ref[k:k + 1, :] * pltpu.roll(ext, back, 0)[hist:hist + rows, :]
        return out

    conv = causal_conv(a_c * a_x, p_hist, conv_a_ref, SHORT_CONV_K)
    y_ref[0, :, 0:w] = (a_b * conv).astype(y_ref.dtype)

    xr = causal_conv(r_x, r_hist, conv_b_ref, LRU_CONV_K) + conv_bb_ref[...]
    xr16 = xr.astype(jnp.bfloat16)
    r = _sigmoid(_dot(xr16, wa_ref[...]) + ba_ref[...])
    i = _sigmoid(_dot(xr16, wx_ref[...]) + bx_ref[...])
    log_a = (-RG_C) * r * _softplus(-lam_ref[...])
    a = jnp.exp(log_a)
    b = jnp.sqrt(1.0 - a * a) * (i * xr)
    hs = _linear_scan(a, b, h_state[0:1, :])
    h_state[0:1, :] = hs[rows - 1:rows, :]
    y_ref[0, :, w:2 * w] = (_gelu_tanh(r_gate) * hs).astype(y_ref.dtype)


def _block_diag(w):
    n_blk, blk, _ = w.shape
    eye = jnp.eye(n_blk, dtype=w.dtype)
    return jnp.einsum('hij,hg->higj', w, eye).reshape(n_blk * blk, n_blk * blk)


def _mixer(x, mod, mix_norm, w_in, conv_a_w, conv_b_w, conv_b_b,
           rg_a_w, rg_a_b, rg_x_w, rg_x_b, rg_lambda):
    bsz, seq, d = x.shape
    rows = MIX_ROWS
    assert seq % rows == 0
    n_in = w_in.shape[1]
    w = LRU_WIDTH
    return pl.pallas_call(
        _mixer_kernel,
        grid=(bsz, seq // rows),
        in_specs=[
            pl.BlockSpec((1, rows, d), lambda b, s: (b, s, 0)),
            pl.BlockSpec((1, N_MOD, d), lambda b, s: (b, 0, 0)),
            _const_spec((1, d)),
            _const_spec((d, n_in)),
            _const_spec((SHORT_CONV_K, CONV_WIDTH)),
            _const_spec((LRU_CONV_K, w)),
            _const_spec((1, w)),
            _const_spec((w, w)),
            _const_spec((1, w)),
            _const_spec((w, w)),
            _const_spec((1, w)),
            _const_spec((1, w)),
        ],
        out_specs=pl.BlockSpec((1, rows, d), lambda b, s: (b, s, 0)),
        out_shape=jax.ShapeDtypeStruct((bsz, seq, d), jnp.bfloat16),
        scratch_shapes=[
            pltpu.VMEM((V7X_SUBLANES, CONV_WIDTH), jnp.float32),
            pltpu.VMEM((V7X_SUBLANES, w), jnp.float32),
            pltpu.VMEM((V7X_SUBLANES, w), jnp.float32),
        ],
        compiler_params=pltpu.CompilerParams(
            dimension_semantics=("arbitrary", "arbitrary"),
            vmem_limit_bytes=V7X_VMEM_LIMIT_BYTES),
        name="l0_mixer",
    )(x, mod, mix_norm.reshape(1, d), w_in.astype(jnp.bfloat16), conv_a_w, conv_b_w,
      conv_b_b.reshape(1, w),
      _block_diag(rg_a_w).astype(jnp.bfloat16), rg_a_b.reshape(1, w),
      _block_diag(rg_x_w).astype(jnp.bfloat16), rg_x_b.reshape(1, w),
      rg_lambda.reshape(1, w))


def _proj_ffn_kernel(x_ref, y_ref, mod_ref, w_out_ref, norm_ref, wg_ref, wu_ref, wd_ref,
                     o_ref, acc_ref):
    x = x_ref[0]
    x1 = x + mod_ref[0, 2:3, :] * _dot(y_ref[0], w_out_ref[...])
    hf = _rms_norm_modulate(x1, norm_ref[...], mod_ref[0, 3:4, :], mod_ref[0, 4:5, :])
    hf16 = hf.astype(jnp.bfloat16)
    for c in range(wg_ref.shape[1] // FFN_CHUNK):
        cols = slice(c * FFN_CHUNK, (c + 1) * FFN_CHUNK)
        g = _dot(hf16, wg_ref[:, cols])
        up = _dot(hf16, wu_ref[:, cols])
        act = (g * _sigmoid(g) * up).astype(jnp.bfloat16)
        part = _dot(act, wd_ref[cols, :])
        if c == 0:
            acc_ref[...] = part
        else:
            acc_ref[...] += part
    o_ref[0] = x1 + mod_ref[0, 5:6, :] * acc_ref[...]


def _proj_ffn(x, y, mod, w_out, ffn_norm, w_gate, w_up, w_down):
    bsz, seq, d = x.shape
    rows = FFN_ROWS
    d_ff = w_gate.shape[1]
    assert seq % rows == 0 and d_ff % FFN_CHUNK == 0
    return pl.pallas_call(
        _proj_ffn_kernel,
        grid=(bsz, seq // rows),
        in_specs=[
            pl.BlockSpec((1, rows, d), lambda b, s: (b, s, 0)),
            pl.BlockSpec((1, rows, d), lambda b, s: (b, s, 0)),
            pl.BlockSpec((1, N_MOD, d), lambda b, s: (b, 0, 0)),
            _const_spec((d, d)),
            _const_spec((1, d)),
            _const_spec((d, d_ff)),
            _const_spec((d, d_ff)),
            _const_spec((d_ff, d)),
        ],
        out_specs=pl.BlockSpec((1, rows, d), lambda b, s: (b, s, 0)),
        out_shape=jax.ShapeDtypeStruct((bsz, seq, d), jnp.float32),
        scratch_shapes=[pltpu.VMEM((rows, d), jnp.float32)],
        compiler_params=pltpu.CompilerParams(
            dimension_semantics=("arbitrary", "arbitrary"),
            vmem_limit_bytes=V7X_VMEM_LIMIT_BYTES),
        name="proj_ffn",
    )(x, y, mod, w_out.astype(jnp.bfloat16), ffn_norm.reshape(1, d),
      w_gate.astype(jnp.bfloat16), w_up.astype(jnp.bfloat16), w_down.astype(jnp.bfloat16))


def _qkv_kernel(x_ref, mod_ref, norm_ref, w_ref, pool_ref, qn_ref, kn_ref,
                q_ref, k_ref, v_ref):
    d = x_ref.shape[2]
    x = x_ref[0]
    h = _rms_norm_modulate(x, norm_ref[...], mod_ref[0, 0:1, :], mod_ref[0, 1:2, :])
    qkv = _dot(h.astype(jnp.bfloat16), w_ref[...])
    q, k, v = qkv[:, 0:d], qkv[:, d:2 * d], qkv[:, 2 * d:3 * d]

    def head_norm(t, g):
        ms = _dot((t * t).astype(jnp.bfloat16), pool_ref[...])
        return t * lax.rsqrt(ms + EPS) * g

    q_ref[0] = (head_norm(q, qn_ref[...]) * (LOG2E * HEAD_DIM ** -0.5)).astype(q_ref.dtype)
    k_ref[0] = head_norm(k, kn_ref[...]).astype(k_ref.dtype)
    v_ref[0] = v.astype(v_ref.dtype)


def _qkv(x, mod, mix_norm, w_qkv, q_norm, k_norm):
    bsz, seq, d = x.shape
    rows = QKV_ROWS
    assert seq % rows == 0
    n_heads = d // HEAD_DIM
    head_of_lane = jnp.arange(d) // HEAD_DIM
    pool = (head_of_lane[:, None] == head_of_lane[None, :]).astype(jnp.bfloat16) * (1.0 / HEAD_DIM)
    tile = pl.BlockSpec((1, rows, d), lambda b, s: (b, s, 0))
    out = jax.ShapeDtypeStruct((bsz, seq, d), jnp.bfloat16)
    return pl.pallas_call(
        _qkv_kernel,
        grid=(bsz, seq // rows),
        in_specs=[
            tile,
            pl.BlockSpec((1, N_MOD, d), lambda b, s: (b, 0, 0)),
            _const_spec((1, d)),
            _const_spec((d, 3 * d)),
            _const_spec((d, d)),
            _const_spec((1, d)),
            _const_spec((1, d)),
        ],
        out_specs=[tile, tile, tile],
        out_shape=[out, out, out],
        compiler_params=pltpu.CompilerParams(
            dimension_semantics=("arbitrary", "arbitrary"),
            vmem_limit_bytes=V7X_VMEM_LIMIT_BYTES),
        name="l1_qkv",
    )(x, mod, mix_norm.reshape(1, d), w_qkv.astype(jnp.bfloat16), pool.astype(jnp.bfloat16),
      jnp.tile(q_norm, n_heads).reshape(1, d), jnp.tile(k_norm, n_heads).reshape(1, d))


MASK_BIAS = -1e30
SKIP_LOG2 = 150.0
EXP2_MAX = 126.0
ITEM_PAD = 2


def _attention_kernel(q_ref, k_ref, v_ref, later_ref, bias_ref, o_ref,
                      q_heads, acc, cs, sp16, m32, w16, rs, cmin, items, *, n_blocks):
    blk = ATT_BLOCK
    n_heads = HEADS_PER_LANE_GROUP
    rows = n_heads * blk
    trash = n_blocks

    lane = lax.broadcasted_iota(jnp.int32, (blk, V7X_LANES), 1)
    for i in range(n_blocks):
        q = q_ref[0, i * blk:(i + 1) * blk, :]
        for h in range(n_heads):
            in_head = (lane >= h * HEAD_DIM) & (lane < (h + 1) * HEAD_DIM)
            q_heads[i, h * blk:(h + 1) * blk, :] = jnp.where(in_head, q, jnp.zeros_like(q))
    q_heads[trash] = jnp.zeros((rows, V7X_LANES), q_heads.dtype)
    acc[trash] = jnp.zeros((rows, V7X_LANES), jnp.float32)
    cs[trash] = jnp.zeros((rows, 1), jnp.float32)

    def item(n, dist):
        qi = items[n]
        kj = jnp.minimum(qi, n_blocks - 1) - dist
        return qi, pl.multiple_of(kj * blk, blk)

    def stage1(n, slot, dist, diag):
        qi, k_row = item(n, dist)
        z = lax.dot_general(q_heads[qi], k_ref[0, pl.ds(k_row, blk), :], (((1,), (1,)), ((), ())),
                            preferred_element_type=jnp.float32)
        if diag:
            z = z + bias_ref[...]
        sp = jnp.maximum(jnp.log(1.0 + jnp.exp2(jnp.minimum(z, EXP2_MAX))) * LOG2E, z)
        m32[slot] = z - sp
        sp16[slot] = sp.astype(jnp.bfloat16)
        rs[slot] = jnp.sum(sp, axis=1, keepdims=True)

    def stage2(n, slot, dist, diag):
        qi, _ = item(n, dist)
        t = m32[slot] - _dot(sp16[slot], later_ref[...])
        c = rs[slot]
        if not diag:
            prev = cs[qi]
            t = t - prev
            c = c + prev
        w16[slot] = jnp.exp2(t).astype(jnp.bfloat16)
        cs[qi] = c
        cmin[pl.ds(qi, 1), :] = jnp.broadcast_to(jnp.min(c, axis=0, keepdims=True), (1, V7X_LANES))

    def stage3(n, slot, dist, diag):
        qi, k_row = item(n, dist)
        pv = _dot(w16[slot], v_ref[0, pl.ds(k_row, blk), :])
        if diag:
            acc[qi] = pv
        else:
            acc[qi] += pv

    def run_wave(n_items, kinds):
        def step(it, slot, stages):
            if 1 in stages:
                stage1(it, slot, *kinds[slot])
            if 2 in stages:
                stage2(it - 1, 1 - slot, *kinds[1 - slot])
            if 3 in stages:
                stage3(it - 2, slot, *kinds[slot])

        step(0, 0, (1,))
        step(1, 1, (1, 2))

        def pair(t, carry):
            it = 2 * t + 2
            step(it, 0, (1, 2, 3))
            step(it + 1, 1, (1, 2, 3))
            return carry

        lax.fori_loop(0, (n_items - 2) // 2, pair, 0)
        step(n_items, 0, (2, 3))
        step(n_items + 1, 1, (3,))

    def list_active(dist):
        low = cmin[0:n_blocks, :]
        block = lax.broadcasted_iota(jnp.int32, low.shape, 0)
        active = (low < SKIP_LOG2) & (block >= dist)
        weights = jnp.where(active, jnp.left_shift(1, block), 0).astype(jnp.float32)
        mask = (jnp.sum(weights) * (1.0 / V7X_LANES)).astype(jnp.int32)
        count = jnp.int32(0)
        for i in range(n_blocks):
            items[count] = i
            count = count + ((mask >> i) & 1)
        for p in range(ITEM_PAD):
            items[count + p] = trash
        return count

    for i in range(n_blocks):
        items[2 * i] = i
        items[2 * i + 1] = i if i > 0 else trash
    run_wave(2 * n_blocks, ((0, True), (1, False)))

    def more_waves(state):
        dist, count = state
        run_wave(count + (count & 1), ((dist, False), (dist, False)))
        return dist + 1, list_active(dist + 1)

    lax.while_loop(lambda state: state[1] > 0, more_waves, (jnp.int32(2), list_active(2)))

    for i in range(n_blocks):
        out = acc[i, 0:blk, :]
        for h in range(1, n_heads):
            out = jnp.where(lane >= h * HEAD_DIM, acc[i, h * blk:(h + 1) * blk, :], out)
        o_ref[0, i * blk:(i + 1) * blk, :] = out.astype(o_ref.dtype)


def _attention(q, k, v):
    bsz, seq, d = q.shape
    blk = ATT_BLOCK
    n_heads = HEADS_PER_LANE_GROUP
    assert seq % blk == 0 and d % V7X_LANES == 0
    n_blocks = seq // blk
    rows = n_heads * blk
    idx = jnp.arange(blk)
    later = (idx[:, None] > idx[None, :]).astype(jnp.bfloat16)
    bias = jnp.where(idx[None, :] < idx[:, None], 0.0, MASK_BIAS).astype(jnp.float32)
    bias = jnp.tile(bias, (n_heads, 1))
    seq_spec = pl.BlockSpec((1, seq, V7X_LANES), lambda b, p: (b, 0, p))
    cmin_rows = pl.cdiv(n_blocks + 1, V7X_SUBLANES) * V7X_SUBLANES
    return pl.pallas_call(
        functools.partial(_attention_kernel, n_blocks=n_blocks),
        grid=(bsz, d // V7X_LANES),
        in_specs=[seq_spec, seq_spec, seq_spec, _const_spec((blk, blk)), _const_spec((rows, blk))],
        out_specs=seq_spec,
        out_shape=jax.ShapeDtypeStruct((bsz, seq, d), jnp.bfloat16),
        scratch_shapes=[
            pltpu.VMEM((n_blocks + 1, rows, V7X_LANES), jnp.bfloat16),
            pltpu.VMEM((n_blocks + 1, rows, V7X_LANES), jnp.float32),
            pltpu.VMEM((n_blocks + 1, rows, 1), jnp.float32),
            pltpu.VMEM((2, rows, blk), jnp.bfloat16),
            pltpu.VMEM((2, rows, blk), jnp.float32),
            pltpu.VMEM((2, rows, blk), jnp.bfloat16),
            pltpu.VMEM((2, rows, 1), jnp.float32),
            pltpu.VMEM((cmin_rows, V7X_LANES), jnp.float32),
            pltpu.SMEM((2 * n_blocks + ITEM_PAD,), jnp.int32),
        ],
        compiler_params=pltpu.CompilerParams(
            dimension_semantics=("arbitrary", "arbitrary"),
            vmem_limit_bytes=V7X_VMEM_LIMIT_BYTES),
        name="sb_attention",
    )(q, k, v, later, bias)


def kernel(x, c, l0_ada_w, l0_ada_b, l0_mix_norm, l0_w_in, l0_conv_a_w, l0_conv_b_w, l0_conv_b_b, l0_rg_a_w, l0_rg_a_b, l0_rg_x_w, l0_rg_x_b, l0_rg_lambda, l0_w_out, l0_ffn_norm, l0_ffn_w_gate, l0_ffn_w_up, l0_ffn_w_down, l1_ada_w, l1_ada_b, l1_mix_norm, l1_w_qkv, l1_q_norm, l1_k_norm, l1_w_out, l1_ffn_norm, l1_ffn_w_gate, l1_ffn_w_up, l1_ffn_w_down):
    mod0 = _ada_modulation(c, l0_ada_w, l0_ada_b)
    mod1 = _ada_modulation(c, l1_ada_w, l1_ada_b)

    y = _mixer(x, mod0, l0_mix_norm, l0_w_in, l0_conv_a_w, l0_conv_b_w, l0_conv_b_b,
               l0_rg_a_w, l0_rg_a_b, l0_rg_x_w, l0_rg_x_b, l0_rg_lambda)
    x = _proj_ffn(x, y, mod0, l0_w_out, l0_ffn_norm, l0_ffn_w_gate, l0_ffn_w_up, l0_ffn_w_down)

    q, k, v = _qkv(x, mod1, l1_mix_norm, l1_w_qkv, l1_q_norm, l1_k_norm)
    o = _attention(q, k, v)
    x = _proj_ffn(x, o, mod1, l1_w_out, l1_ffn_norm, l1_ffn_w_gate, l1_ffn_w_up, l1_ffn_w_down)
    return x
```

```python
import functools
import math

import jax
import jax.numpy as jnp
from jax import lax
from jax.experimental import pallas as pl
from jax.experimental.pallas import tpu as pltpu

D_MODEL = 1024
HEAD_DIM = 64
CONV_WIDTH = D_MODEL // 2
LRU_WIDTH = D_MODEL // 2
SHORT_CONV_K = 3
LRU_CONV_K = 4
RG_C = 8.0
N_MOD = 6
EPS = 1e-6
LOG2E = math.log2(math.e)

V7X_LANES = 128
V7X_SUBLANES = 8
V7X_MXU_DIM = 256
V7X_VMEM_LIMIT_BYTES = 56 * 1024 * 1024

HEADS_PER_LANE_GROUP = V7X_LANES // HEAD_DIM

MIX_ROWS = 512
FFN_ROWS = 512
QKV_ROWS = 512
ATT_BLOCK = 256
FFN_CHUNK = V7X_MXU_DIM
POOL_LANES = V7X_MXU_DIM
ADA_COLS = 1536


def _const_spec(shape):
    zeros = (0,) * len(shape)
    return pl.BlockSpec(shape, lambda *_: zeros, pipeline_mode=pl.Buffered(1))


def _sigmoid(x):
    return 1.0 / (1.0 + jnp.exp(-x))


def _softplus(x):
    return jnp.maximum(x, 0.0) + jnp.log(1.0 + jnp.exp(-jnp.abs(x)))


def _gelu_tanh(x):
    c = math.sqrt(2.0 / math.pi)
    return 0.5 * x * (1.0 + jnp.tanh(c * (x + 0.044715 * (x * x * x))))


def _rms_norm_modulate(x, g, shift, scale):
    ms = jnp.mean(x * x, axis=-1, keepdims=True)
    return x * lax.rsqrt(ms + EPS) * (g * (1.0 + scale)) + shift


def _split_bf16(x):
    hi = x.astype(jnp.bfloat16)
    lo = (x - hi.astype(jnp.float32)).astype(jnp.bfloat16)
    return hi, lo


def _dot(a, b):
    return jnp.dot(a, b, preferred_element_type=jnp.float32)


def _ada_kernel(c_ref, w_ref, b_ref, o_ref):
    c = c_ref[...]
    s = c * _sigmoid(c)
    s_hi, s_lo = _split_bf16(s)
    w_hi, w_lo = _split_bf16(w_ref[...])
    acc = _dot(s_hi, w_hi) + _dot(s_hi, w_lo) + _dot(s_lo, w_hi)
    o_ref[...] = acc + b_ref[...]


def _ada_modulation(c, ada_w, ada_b):
    bsz, d = c.shape
    n = ada_w.shape[1]
    assert n % ADA_COLS == 0
    out = pl.pallas_call(
        _ada_kernel,
        grid=(n // ADA_COLS,),
        in_specs=[
            pl.BlockSpec((bsz, d), lambda j: (0, 0)),
            pl.BlockSpec((d, ADA_COLS), lambda j: (0, j)),
            pl.BlockSpec((1, ADA_COLS), lambda j: (0, j)),
        ],
        out_specs=pl.BlockSpec((bsz, ADA_COLS), lambda j: (0, j)),
        out_shape=jax.ShapeDtypeStruct((bsz, n), jnp.float32),
        compiler_params=pltpu.CompilerParams(dimension_semantics=("arbitrary",)),
        name="ada_modulation",
    )(c, ada_w, ada_b.reshape(1, n))
    return out.reshape(bsz, N_MOD, d)


def _linear_scan(a, b, h0):
    n, width = a.shape
    group = V7X_SUBLANES
    row = lax.broadcasted_iota(jnp.int32, (group, width), 0)
    h = h0
    out = []
    for g in range(n // group):
        ag = a[g * group:(g + 1) * group, :]
        bg = b[g * group:(g + 1) * group, :]
        d = 1
        while d < group:
            a_prev = jnp.where(row >= d, pltpu.roll(ag, d, 0), 1.0)
            b_prev = jnp.where(row >= d, pltpu.roll(bg, d, 0), 0.0)
            bg = ag * b_prev + bg
            ag = ag * a_prev
            d *= 2
        hg = ag * h + bg
        h = hg[group - 1:group, :]
        out.append(hg)
    return jnp.concatenate(out, axis=0)


def _mixer_kernel(x_ref, mod_ref, norm_ref, w_in_ref, conv_a_ref, conv_b_ref, conv_bb_ref,
                  wa_ref, ba_ref, wx_ref, bx_ref, lam_ref, y_ref,
                  p_hist, r_hist, h_state):
    rows = x_ref.shape[1]
    hist = V7X_SUBLANES
    first = pl.program_id(1) == 0

    @pl.when(first)
    def _():
        p_hist[...] = jnp.zeros_like(p_hist)
        r_hist[...] = jnp.zeros_like(r_hist)
        h_state[...] = jnp.zeros_like(h_state)

    x = x_ref[0]
    h = _rms_norm_modulate(x, norm_ref[...], mod_ref[0, 0:1, :], mod_ref[0, 1:2, :])
    u = _dot(h.astype(jnp.bfloat16), w_in_ref[...])
    w = CONV_WIDTH
    a_b, a_c, a_x = u[:, 0:w], u[:, w:2 * w], u[:, 2 * w:3 * w]
    r_gate, r_x = u[:, 3 * w:4 * w], u[:, 4 * w:5 * w]

    def causal_conv(v, hist_ref, taps_ref, n_taps):
        ext = jnp.concatenate([hist_ref[...], v], axis=0)
        hist_ref[...] = v[rows - hist:rows, :]
        out = taps_ref[n_taps - 1:n_taps, :] * v
        for k in range(n_taps - 1):
            back = n_taps - 1 - k
            out = out + taps_ref[k:k + 1, :] * pltpu.roll(ext, back, 0)[hist:hist + rows, :]
        return out

    conv = causal_conv(a_c * a_x, p_hist, conv_a_ref, SHORT_CONV_K)
    y_ref[0, :, 0:w] = (a_b * conv).astype(y_ref.dtype)

    xr = causal_conv(r_x, r_hist, conv_b_ref, LRU_CONV_K) + conv_bb_ref[...]
    xr16 = xr.astype(jnp.bfloat16)
    r = _sigmoid(_dot(xr16, wa_ref[...]) + ba_ref[...])
    i = _sigmoid(_dot(xr16, wx_ref[...]) + bx_ref[...])
    log_a = (-RG_C) * r * _softplus(-lam_ref[...])
    a = jnp.exp(log_a)
    b = jnp.sqrt(1.0 - a * a) * (i * xr)
    hs = _linear_scan(a, b, h_state[0:1, :])
    h_state[0:1, :] = hs[rows - 1:rows, :]
    y_ref[0, :, w:2 * w] = (_gelu_tanh(r_gate) * hs).astype(y_ref.dtype)


def _block_diag(w):
    n_blk, blk, _ = w.shape
    eye = jnp.eye(n_blk, dtype=w.dtype)
    return jnp.einsum('hij,hg->higj', w, eye).reshape(n_blk * blk, n_blk * blk)


def _mixer(x, mod, mix_norm, w_in, conv_a_w, conv_b_w, conv_b_b,
           rg_a_w, rg_a_b, rg_x_w, rg_x_b, rg_lambda):
    bsz, seq, d = x.shape
    rows = MIX_ROWS
    assert seq % rows == 0
    n_in = w_in.shape[1]
    w = LRU_WIDTH
    return pl.pallas_call(
        _mixer_kernel,
        grid=(bsz, seq // rows),
        in_specs=[
            pl.BlockSpec((1, rows, d), lambda b, s: (b, s, 0)),
            pl.BlockSpec((1, N_MOD, d), lambda b, s: (b, 0, 0)),
            _const_spec((1, d)),
            _const_spec((d, n_in)),
            _const_spec((SHORT_CONV_K, CONV_WIDTH)),
            _const_spec((LRU_CONV_K, w)),
            _const_spec((1, w)),
            _const_spec((w, w)),
            _const_spec((1, w)),
            _const_spec((w, w)),
            _const_spec((1, w)),
            _const_spec((1, w)),
        ],
        out_specs=pl.BlockSpec((1, rows, d), lambda b, s: (b, s, 0)),
        out_shape=jax.ShapeDtypeStruct((bsz, seq, d), jnp.bfloat16),
        scratch_shapes=[
            pltpu.VMEM((V7X_SUBLANES, CONV_WIDTH), jnp.float32),
            pltpu.VMEM((V7X_SUBLANES, w), jnp.float32),
            pltpu.VMEM((V7X_SUBLANES, w), jnp.float32),
        ],
        compiler_params=pltpu.CompilerParams(
            dimension_semantics=("arbitrary", "arbitrary"),
            vmem_limit_bytes=V7X_VMEM_LIMIT_BYTES),
        name="l0_mixer",
    )(x, mod, mix_norm.reshape(1, d), w_in.astype(jnp.bfloat16), conv_a_w, conv_b_w,
      conv_b_b.reshape(1, w),
      _block_diag(rg_a_w).astype(jnp.bfloat16), rg_a_b.reshape(1, w),
      _block_diag(rg_x_w).astype(jnp.bfloat16), rg_x_b.reshape(1, w),
      rg_lambda.reshape(1, w))


def _proj_ffn_kernel(x_ref, y_ref, mod_ref, w_out_ref, norm_ref, wg_ref, wu_ref, wd_ref,
                     o_ref, acc_ref):
    x = x_ref[0]
    x1 = x + mod_ref[0, 2:3, :] * _dot(y_ref[0], w_out_ref[...])
    hf = _rms_norm_modulate(x1, norm_ref[...], mod_ref[0, 3:4, :], mod_ref[0, 4:5, :])
    hf16 = hf.astype(jnp.bfloat16)
    for c in range(wg_ref.shape[1] // FFN_CHUNK):
        cols = slice(c * FFN_CHUNK, (c + 1) * FFN_CHUNK)
        g = _dot(hf16, wg_ref[:, cols])
        up = _dot(hf16, wu_ref[:, cols])
        act = (g * _sigmoid(g) * up).astype(jnp.bfloat16)
        part = _dot(act, wd_ref[cols, :])
        if c == 0:
            acc_ref[...] = part
        else:
            acc_ref[...] += part
    o_ref[0] = x1 + mod_ref[0, 5:6, :] * acc_ref[...]


def _proj_ffn(x, y, mod, w_out, ffn_norm, w_gate, w_up, w_down):
    bsz, seq, d = x.shape
    rows = FFN_ROWS
    d_ff = w_gate.shape[1]
    assert seq % rows == 0 and d_ff % FFN_CHUNK == 0
    return pl.pallas_call(
        _proj_ffn_kernel,
        grid=(bsz, seq // rows),
        in_specs=[
            pl.BlockSpec((1, rows, d), lambda b, s: (b, s, 0)),
            pl.BlockSpec((1, rows, d), lambda b, s: (b, s, 0)),
            pl.BlockSpec((1, N_MOD, d), lambda b, s: (b, 0, 0)),
            _const_spec((d, d)),
            _const_spec((1, d)),
            _const_spec((d, d_ff)),
            _const_spec((d, d_ff)),
            _const_spec((d_ff, d)),
        ],
        out_specs=pl.BlockSpec((1, rows, d), lambda b, s: (b, s, 0)),
        out_shape=jax.ShapeDtypeStruct((bsz, seq, d), jnp.float32),
        scratch_shapes=[pltpu.VMEM((rows, d), jnp.float32)],
        compiler_params=pltpu.CompilerParams(
            dimension_semantics=("arbitrary", "arbitrary"),
            vmem_limit_bytes=V7X_VMEM_LIMIT_BYTES),
        name="proj_ffn",
    )(x, y, mod, w_out.astype(jnp.bfloat16), ffn_norm.reshape(1, d),
      w_gate.astype(jnp.bfloat16), w_up.astype(jnp.bfloat16), w_down.astype(jnp.bfloat16))


def _qkv_kernel(x_ref, mod_ref, norm_ref, w_ref, pool_ref, qn_ref, kn_ref,
                q_ref, k_ref, v_ref):
    d = x_ref.shape[2]
    x = x_ref[0]
    h = _rms_norm_modulate(x, norm_ref[...], mod_ref[0, 0:1, :], mod_ref[0, 1:2, :])
    qkv = _dot(h.astype(jnp.bfloat16), w_ref[...])
    q, k, v = qkv[:, 0:d], qkv[:, d:2 * d], qkv[:, 2 * d:3 * d]

    def head_norm(t, g_ref, scale, o_ref):
        for c in range(d // POOL_LANES):
            cols = slice(c * POOL_LANES, (c + 1) * POOL_LANES)
            tc = t[:, cols]
            ms = _dot((tc * tc).astype(jnp.bfloat16), pool_ref[...])
            o_ref[0, :, cols] = (tc * lax.rsqrt(ms + EPS) * (g_ref[:, cols] * scale)).astype(o_ref.dtype)

    head_norm(q, qn_ref, LOG2E * HEAD_DIM ** -0.5, q_ref)
    head_norm(k, kn_ref, 1.0, k_ref)
    v_ref[0] = v.astype(v_ref.dtype)


def _qkv(x, mod, mix_norm, w_qkv, q_norm, k_norm):
    bsz, seq, d = x.shape
    rows = QKV_ROWS
    assert seq % rows == 0
    n_heads = d // HEAD_DIM
    head_of_lane = jnp.arange(POOL_LANES) // HEAD_DIM
    pool = (head_of_lane[:, None] == head_of_lane[None, :]).astype(jnp.bfloat16) * (1.0 / HEAD_DIM)
    tile = pl.BlockSpec((1, rows, d), lambda b, s: (b, s, 0))
    out = jax.ShapeDtypeStruct((bsz, seq, d), jnp.bfloat16)
    return pl.pallas_call(
        _qkv_kernel,
        grid=(bsz, seq // rows),
        in_specs=[
            tile,
            pl.BlockSpec((1, N_MOD, d), lambda b, s: (b, 0, 0)),
            _const_spec((1, d)),
            _const_spec((d, 3 * d)),
            _const_spec((POOL_LANES, POOL_LANES)),
            _const_spec((1, d)),
            _const_spec((1, d)),
        ],
        out_specs=[tile, tile, tile],
        out_shape=[out, out, out],
        compiler_params=pltpu.CompilerParams(
            dimension_semantics=("arbitrary", "arbitrary"),
            vmem_limit_bytes=V7X_VMEM_LIMIT_BYTES),
        name="l1_qkv",
    )(x, mod, mix_norm.reshape(1, d), w_qkv.astype(jnp.bfloat16), pool.astype(jnp.bfloat16),
      jnp.tile(q_norm, n_heads).reshape(1, d), jnp.tile(k_norm, n_heads).reshape(1, d))


MASK_BIAS = -1e30
SKIP_LOG2 = 150.0
EXP2_MAX = 126.0
ITEM_PAD = 2


def _attention_kernel(q_ref, k_ref, v_ref, later_ref, bias_ref, o_ref,
                      q_heads, acc, cs, sp16, m32, w16, rs, cmin, items, *, n_blocks):
    blk = ATT_BLOCK
    n_heads = HEADS_PER_LANE_GROUP
    rows = n_heads * blk
    trash = n_blocks

    lane = lax.broadcasted_iota(jnp.int32, (blk, V7X_LANES), 1)
    for i in range(n_blocks):
        q = q_ref[0, i * blk:(i + 1) * blk, :]
        for h in range(n_heads):
            in_head = (lane >= h * HEAD_DIM) & (lane < (h + 1) * HEAD_DIM)
            q_heads[i, h * blk:(h + 1) * blk, :] = jnp.where(in_head, q, jnp.zeros_like(q))
    q_heads[trash] = jnp.zeros((rows, V7X_LANES), q_heads.dtype)
    acc[trash] = jnp.zeros((rows, V7X_LANES), jnp.float32)
    cs[trash] = jnp.zeros((rows, 1), jnp.float32)

    def item(n, dist):
        qi = items[n]
        kj = jnp.minimum(qi, n_blocks - 1) - dist
        return qi, pl.multiple_of(kj * blk, blk)

    def stage1(n, slot, dist, diag):
        qi, k_row = item(n, dist)
        z = lax.dot_general(q_heads[qi], k_ref[0, pl.ds(k_row, blk), :], (((1,), (1,)), ((), ())),
                            preferred_element_type=jnp.float32)
        if diag:
            z = z + bias_ref[...]
        sp = jnp.maximum(jnp.log(1.0 + jnp.exp2(jnp.minimum(z, EXP2_MAX))) * LOG2E, z)
        m32[slot] = z - sp
        sp16[slot] = sp.astype(jnp.bfloat16)
        rs[slot] = jnp.sum(sp, axis=1, keepdims=True)

    def stage2(n, slot, dist, diag):
        qi, _ = item(n, dist)
        t = m32[slot] - _dot(sp16[slot], later_ref[...])
        c = rs[slot]
        if not diag:
            prev = cs[qi]
            t = t - prev
            c = c + prev
        w16[slot] = jnp.exp2(t).astype(jnp.bfloat16)
        cs[qi] = c
        cmin[pl.ds(qi, 1), :] = jnp.broadcast_to(jnp.min(c, axis=0, keepdims=True), (1, V7X_LANES))

    def stage3(n, slot, dist, diag):
        qi, k_row = item(n, dist)
        pv = _dot(w16[slot], v_ref[0, pl.ds(k_row, blk), :])
        if diag:
            acc[qi] = pv
        else:
            acc[qi] += pv

    def run_wave(n_items, kinds):
        def step(it, slot, stages):
            if 1 in stages:
                stage1(it, slot, *kinds[slot])
            if 2 in stages:
                stage2(it - 1, 1 - slot, *kinds[1 - slot])
            if 3 in stages:
                stage3(it - 2, slot, *kinds[slot])

        step(0, 0, (1,))
        step(1, 1, (1, 2))

        def pair(t, carry):
            it = 2 * t + 2
            step(it, 0, (1, 2, 3))
            step(it + 1, 1, (1, 2, 3))
            return carry

        lax.fori_loop(0, (n_items - 2) // 2, pair, 0)
        step(n_items, 0, (2, 3))
        step(n_items + 1, 1, (3,))

    def list_active(dist):
        low = cmin[0:n_blocks, :]
        block = lax.broadcasted_iota(jnp.int32, low.shape, 0)
        active = (low < SKIP_LOG2) & (block >= dist)
        weights = jnp.where(active, jnp.left_shift(1, block), 0).astype(jnp.float32)
        mask = (jnp.sum(weights) * (1.0 / V7X_LANES)).astype(jnp.int32)
        count = jnp.int32(0)
        for i in range(n_blocks):
            items[count] = i
            count = count + ((mask >> i) & 1)
        for p in range(ITEM_PAD):
            items[count + p] = trash
        return count

    for i in range(n_blocks):
        items[2 * i] = i
        items[2 * i + 1] = i if i > 0 else trash
    run_wave(2 * n_blocks, ((0, True), (1, False)))

    def more_waves(state):
        dist, count = state
        run_wave(count + (count & 1), ((dist, False), (dist, False)))
        return dist + 1, list_active(dist + 1)

    lax.while_loop(lambda state: state[1] > 0, more_waves, (jnp.int32(2), list_active(2)))

    for i in range(n_blocks):
        out = acc[i, 0:blk, :]
        for h in range(1, n_heads):
            out = jnp.where(lane >= h * HEAD_DIM, acc[i, h * blk:(h + 1) * blk, :], out)
        o_ref[0, i * blk:(i + 1) * blk, :] = out.astype(o_ref.dtype)


def _attention(q, k, v):
    bsz, seq, d = q.shape
    blk = ATT_BLOCK
    n_heads = HEADS_PER_LANE_GROUP
    assert seq % blk == 0 and d % V7X_LANES == 0
    n_blocks = seq // blk
    rows = n_heads * blk
    idx = jnp.arange(blk)
    later = (idx[:, None] > idx[None, :]).astype(jnp.bfloat16)
    bias = jnp.where(idx[None, :] < idx[:, None], 0.0, MASK_BIAS).astype(jnp.float32)
    bias = jnp.tile(bias, (n_heads, 1))
    seq_spec = pl.BlockSpec((1, seq, V7X_LANES), lambda b, p: (b, 0, p))
    cmin_rows = pl.cdiv(n_blocks + 1, V7X_SUBLANES) * V7X_SUBLANES
    return pl.pallas_call(
        functools.partial(_attention_kernel, n_blocks=n_blocks),
        grid=(bsz, d // V7X_LANES),
        in_specs=[seq_spec, seq_spec, seq_spec, _const_spec((blk, blk)), _const_spec((rows, blk))],
        out_specs=seq_spec,
        out_shape=jax.ShapeDtypeStruct((bsz, seq, d), jnp.bfloat16),
        scratch_shapes=[
            pltpu.VMEM((n_blocks + 1, rows, V7X_LANES), jnp.bfloat16),
            pltpu.VMEM((n_blocks + 1, rows, V7X_LANES), jnp.float32),
            pltpu.VMEM((n_blocks + 1, rows, 1), jnp.float32),
            pltpu.VMEM((2, rows, blk), jnp.bfloat16),
            pltpu.VMEM((2, rows, blk), jnp.float32),
            pltpu.VMEM((2, rows, blk), jnp.bfloat16),
            pltpu.VMEM((2, rows, 1), jnp.float32),
            pltpu.VMEM((cmin_rows, V7X_LANES), jnp.float32),
            pltpu.SMEM((2 * n_blocks + ITEM_PAD,), jnp.int32),
        ],
        compiler_params=pltpu.CompilerParams(
            dimension_semantics=("arbitrary", "arbitrary"),
            vmem_limit_bytes=V7X_VMEM_LIMIT_BYTES),
        name="sb_attention",
    )(q, k, v, later, bias)


def kernel(x, c, l0_ada_w, l0_ada_b, l0_mix_norm, l0_w_in, l0_conv_a_w, l0_conv_b_w, l0_conv_b_b, l0_rg_a_w, l0_rg_a_b, l0_rg_x_w, l0_rg_x_b, l0_rg_lambda, l0_w_out, l0_ffn_norm, l0_ffn_w_gate, l0_ffn_w_up, l0_ffn_w_down, l1_ada_w, l1_ada_b, l1_mix_norm, l1_w_qkv, l1_q_norm, l1_k_norm, l1_w_out, l1_ffn_norm, l1_ffn_w_gate, l1_ffn_w_up, l1_ffn_w_down):
    mod0 = _ada_modulation(c, l0_ada_w, l0_ada_b)
    mod1 = _ada_modulation(c, l1_ada_w, l1_ada_b)

    y = _mixer(x, mod0, l0_mix_norm, l0_w_in, l0_conv_a_w, l0_conv_b_w, l0_conv_b_b,
               l0_rg_a_w, l0_rg_a_b, l0_rg_x_w, l0_rg_x_b, l0_rg_lambda)
    x = _proj_ffn(x, y, mod0, l0_w_out, l0_ffn_norm, l0_ffn_w_gate, l0_ffn_w_up, l0_ffn_w_down)

    q, k, v = _qkv(x, mod1, l1_mix_norm, l1_w_qkv, l1_q_norm, l1_k_norm)
    o = _attention(q, k, v)
    x = _proj_ffn(x, o, mod1, l1_w_out, l1_ffn_norm, l1_ffn_w_gate, l1_ffn_w_up, l1_ffn_w_down)
    return x
```

```python
import functools
import math

import jax
import jax.numpy as jnp
from jax import lax
from jax.experimental import pallas as pl
from jax.experimental.pallas import tpu as pltpu

D_MODEL = 1024
HEAD_DIM = 64
CONV_WIDTH = D_MODEL // 2
LRU_WIDTH = D_MODEL // 2
SHORT_CONV_K = 3
LRU_CONV_K = 4
RG_C = 8.0
N_MOD = 6
EPS = 1e-6
LOG2E = math.log2(math.e)

V7X_LANES = 128
V7X_SUBLANES = 8
V7X_MXU_DIM = 256
V7X_VMEM_LIMIT_BYTES = 56 * 1024 * 1024

HEADS_PER_LANE_GROUP = V7X_LANES // HEAD_DIM

MIX_ROWS = 512
FFN_ROWS = 1024
QKV_ROWS = 512
ATT_BLOCK = 256
FFN_CHUNK = V7X_MXU_DIM
POOL_LANES = V7X_MXU_DIM
ADA_COLS = 1536


def _const_spec(shape):
    zeros = (0,) * len(shape)
    return pl.BlockSpec(shape, lambda *_: zeros, pipeline_mode=pl.Buffered(1))


def _sigmoid(x):
    return 1.0 / (1.0 + jnp.exp(-x))


def _softplus(x):
    return jnp.maximum(x, 0.0) + jnp.log(1.0 + jnp.exp(-jnp.abs(x)))


def _gelu_tanh(x):
    c = math.sqrt(2.0 / math.pi)
    return 0.5 * x * (1.0 + jnp.tanh(c * (x + 0.044715 * (x * x * x))))


def _rms_norm_modulate(x, g, shift, scale):
    ms = jnp.mean(x * x, axis=-1, keepdims=True)
    return x * lax.rsqrt(ms + EPS) * (g * (1.0 + scale)) + shift


def _split_bf16(x):
    hi = x.astype(jnp.bfloat16)
    lo = (x - hi.astype(jnp.float32)).astype(jnp.bfloat16)
    return hi, lo


def _dot(a, b):
    return jnp.dot(a, b, preferred_element_type=jnp.float32)


def _ada_kernel(c_ref, w_ref, b_ref, o_ref):
    c = c_ref[...]
    s = c * _sigmoid(c)
    s_hi, s_lo = _split_bf16(s)
    w_hi, w_lo = _split_bf16(w_ref[...])
    acc = _dot(s_hi, w_hi) + _dot(s_hi, w_lo) + _dot(s_lo, w_hi)
    o_ref[...] = acc + b_ref[...]


def _ada_modulation(c, ada_w, ada_b):
    bsz, d = c.shape
    n = ada_w.shape[1]
    assert n % ADA_COLS == 0
    out = pl.pallas_call(
        _ada_kernel,
        grid=(n // ADA_COLS,),
        in_specs=[
            pl.BlockSpec((bsz, d), lambda j: (0, 0)),
            pl.BlockSpec((d, ADA_COLS), lambda j: (0, j)),
            pl.BlockSpec((1, ADA_COLS), lambda j: (0, j)),
        ],
        out_specs=pl.BlockSpec((bsz, ADA_COLS), lambda j: (0, j)),
        out_shape=jax.ShapeDtypeStruct((bsz, n), jnp.float32),
        compiler_params=pltpu.CompilerParams(dimension_semantics=("arbitrary",)),
        name="ada_modulation",
    )(c, ada_w, ada_b.reshape(1, n))
    return out.reshape(bsz, N_MOD, d)


def _linear_scan(a, b, h0):
    n, width = a.shape
    group = V7X_SUBLANES
    row = lax.broadcasted_iota(jnp.int32, (group, width), 0)
    h = h0
    out = []
    for g in range(n // group):
        ag = a[g * group:(g + 1) * group, :]
        bg = b[g * group:(g + 1) * group, :]
        d = 1
        while d < group:
            a_prev = jnp.where(row >= d, pltpu.roll(ag, d, 0), 1.0)
            b_prev = jnp.where(row >= d, pltpu.roll(bg, d, 0), 0.0)
            bg = ag * b_prev + bg
            ag = ag * a_prev
            d *= 2
        hg = ag * h + bg
        h = hg[group - 1:group, :]
        out.append(hg)
    return jnp.concatenate(out, axis=0)


def _mixer_kernel(x_ref, mod_ref, norm_ref, w_in_ref, conv_a_ref, conv_b_ref, conv_bb_ref,
                  wa_ref, ba_ref, wx_ref, bx_ref, lam_ref, y_ref,
                  p_hist, r_hist, h_state):
    rows = x_ref.shape[1]
    hist = V7X_SUBLANES
    first = pl.program_id(1) == 0

    @pl.when(first)
    def _():
        p_hist[...] = jnp.zeros_like(p_hist)
        r_hist[...] = jnp.zeros_like(r_hist)
        h_state[...] = jnp.zeros_like(h_state)

    x = x_ref[0]
    h = _rms_norm_modulate(x, norm_ref[...], mod_ref[0, 0:1, :], mod_ref[0, 1:2, :])
    u = _dot(h.astype(jnp.bfloat16), w_in_ref[...])
    w = CONV_WIDTH
    a_b, a_c, a_x = u[:, 0:w], u[:, w:2 * w], u[:, 2 * w:3 * w]
    r_gate, r_x = u[:, 3 * w:4 * w], u[:, 4 * w:5 * w]

    def causal_conv(v, hist_ref, taps_ref, n_taps):
        ext = jnp.concatenate([hist_ref[...], v], axis=0)
        hist_ref[...] = v[rows - hist:rows, :]
        out = taps_ref[n_taps - 1:n_taps, :] * v
        for k in range(n_taps - 1):
            back = n_taps - 1 - k
            out = out + taps_ref[k:k + 1, :] * pltpu.roll(ext, back, 0)[hist:hist + rows, :]
        return out

    conv = causal_conv(a_c * a_x, p_hist, conv_a_ref, SHORT_CONV_K)
    y_ref[0, :, 0:w] = (a_b * conv).astype(y_ref.dtype)

    xr = causal_conv(r_x, r_hist, conv_b_ref, LRU_CONV_K) + conv_bb_ref[...]
    xr16 = xr.astype(jnp.bfloat16)
    r = _sigmoid(_dot(xr16, wa_ref[...]) + ba_ref[...])
    i = _sigmoid(_dot(xr16, wx_ref[...]) + bx_ref[...])
    log_a = (-RG_C) * r * _softplus(-lam_ref[...])
    a = jnp.exp(log_a)
    b = jnp.sqrt(1.0 - a * a) * (i * xr)
    hs = _linear_scan(a, b, h_state[0:1, :])
    h_state[0:1, :] = hs[rows - 1:rows, :]
    y_ref[0, :, w:2 * w] = (_gelu_tanh(r_gate) * hs).astype(y_ref.dtype)


def _block_diag(w):
    n_blk, blk, _ = w.shape
    eye = jnp.eye(n_blk, dtype=w.dtype)
    return jnp.einsum('hij,hg->higj', w, eye).reshape(n_blk * blk, n_blk * blk)


def _mixer(x, mod, mix_norm, w_in, conv_a_w, conv_b_w, conv_b_b,
           rg_a_w, rg_a_b, rg_x_w, rg_x_b, rg_lambda):
    bsz, seq, d = x.shape
    rows = MIX_ROWS
    assert seq % rows == 0
    n_in = w_in.shape[1]
    w = LRU_WIDTH
    return pl.pallas_call(
        _mixer_kernel,
        grid=(bsz, seq // rows),
        in_specs=[
            pl.BlockSpec((1, rows, d), lambda b, s: (b, s, 0)),
            pl.BlockSpec((1, N_MOD, d), lambda b, s: (b, 0, 0)),
            _const_spec((1, d)),
            _const_spec((d, n_in)),
            _const_spec((SHORT_CONV_K, CONV_WIDTH)),
            _const_spec((LRU_CONV_K, w)),
            _const_spec((1, w)),
            _const_spec((w, w)),
            _const_spec((1, w)),
            _const_spec((w, w)),
            _const_spec((1, w)),
            _const_spec((1, w)),
        ],
        out_specs=pl.BlockSpec((1, rows, d), lambda b, s: (b, s, 0)),
        out_shape=jax.ShapeDtypeStruct((bsz, seq, d), jnp.bfloat16),
        scratch_shapes=[
            pltpu.VMEM((V7X_SUBLANES, CONV_WIDTH), jnp.float32),
            pltpu.VMEM((V7X_SUBLANES, w), jnp.float32),
            pltpu.VMEM((V7X_SUBLANES, w), jnp.float32),
        ],
        compiler_params=pltpu.CompilerParams(
            dimension_semantics=("arbitrary", "arbitrary"),
            vmem_limit_bytes=V7X_VMEM_LIMIT_BYTES),
        name="l0_mixer",
    )(x, mod, mix_norm.reshape(1, d), w_in.astype(jnp.bfloat16), conv_a_w, conv_b_w,
      conv_b_b.reshape(1, w),
      _block_diag(rg_a_w).astype(jnp.bfloat16), rg_a_b.reshape(1, w),
      _block_diag(rg_x_w).astype(jnp.bfloat16), rg_x_b.reshape(1, w),
      rg_lambda.reshape(1, w))


def _proj_ffn_kernel(x_ref, y_ref, mod_ref, w_out_ref, norm_ref, wg_ref, wu_ref, wd_ref,
                     o_ref, acc_ref):
    x = x_ref[0]
    x1 = x + mod_ref[0, 2:3, :] * _dot(y_ref[0], w_out_ref[...])
    hf = _rms_norm_modulate(x1, norm_ref[...], mod_ref[0, 3:4, :], mod_ref[0, 4:5, :])
    hf16 = hf.astype(jnp.bfloat16)
    for c in range(wg_ref.shape[1] // FFN_CHUNK):
        cols = slice(c * FFN_CHUNK, (c + 1) * FFN_CHUNK)
        g = _dot(hf16, wg_ref[:, cols])
        up = _dot(hf16, wu_ref[:, cols])
        act = (g * _sigmoid(g) * up).astype(jnp.bfloat16)
        part = _dot(act, wd_ref[cols, :])
        if c == 0:
            acc_ref[...] = part
        else:
            acc_ref[...] += part
    o_ref[0] = x1 + mod_ref[0, 5:6, :] * acc_ref[...]


def _proj_ffn(x, y, mod, w_out, ffn_norm, w_gate, w_up, w_down):
    bsz, seq, d = x.shape
    rows = FFN_ROWS
    d_ff = w_gate.shape[1]
    assert seq % rows == 0 and d_ff % FFN_CHUNK == 0
    return pl.pallas_call(
        _proj_ffn_kernel,
        grid=(bsz, seq // rows),
        in_specs=[
            pl.BlockSpec((1, rows, d), lambda b, s: (b, s, 0)),
            pl.BlockSpec((1, rows, d), lambda b, s: (b, s, 0)),
            pl.BlockSpec((1, N_MOD, d), lambda b, s: (b, 0, 0)),
            _const_spec((d, d)),
            _const_spec((1, d)),
            _const_spec((d, d_ff)),
            _const_spec((d, d_ff)),
            _const_spec((d_ff, d)),
        ],
        out_specs=pl.BlockSpec((1, rows, d), lambda b, s: (b, s, 0)),
        out_shape=jax.ShapeDtypeStruct((bsz, seq, d), jnp.float32),
        scratch_shapes=[pltpu.VMEM((rows, d), jnp.float32)],
        compiler_params=pltpu.CompilerParams(
            dimension_semantics=("arbitrary", "arbitrary"),
            vmem_limit_bytes=V7X_VMEM_LIMIT_BYTES),
        name="proj_ffn",
    )(x, y, mod, w_out.astype(jnp.bfloat16), ffn_norm.reshape(1, d),
      w_gate.astype(jnp.bfloat16), w_up.astype(jnp.bfloat16), w_down.astype(jnp.bfloat16))


def _qkv_kernel(x_ref, mod_ref, norm_ref, w_ref, pool_ref, qn_ref, kn_ref,
                q_ref, k_ref, v_ref):
    d = x_ref.shape[2]
    x = x_ref[0]
    h = _rms_norm_modulate(x, norm_ref[...], mod_ref[0, 0:1, :], mod_ref[0, 1:2, :])
    qkv = _dot(h.astype(jnp.bfloat16), w_ref[...])
    q, k, v = qkv[:, 0:d], qkv[:, d:2 * d], qkv[:, 2 * d:3 * d]

    def head_norm(t, g_ref, scale, o_ref):
        for c in range(d // POOL_LANES):
            cols = slice(c * POOL_LANES, (c + 1) * POOL_LANES)
            tc = t[:, cols]
            ms = _dot((tc * tc).astype(jnp.bfloat16), pool_ref[...])
            o_ref[0, :, cols] = (tc * lax.rsqrt(ms + EPS) * (g_ref[:, cols] * scale)).astype(o_ref.dtype)

    head_norm(q, qn_ref, LOG2E * HEAD_DIM ** -0.5, q_ref)
    head_norm(k, kn_ref, 1.0, k_ref)
    v_ref[0] = v.astype(v_ref.dtype)


def _qkv(x, mod, mix_norm, w_qkv, q_norm, k_norm):
    bsz, seq, d = x.shape
    rows = QKV_ROWS
    assert seq % rows == 0
    n_heads = d // HEAD_DIM
    head_of_lane = jnp.arange(POOL_LANES) // HEAD_DIM
    pool = (head_of_lane[:, None] == head_of_lane[None, :]).astype(jnp.bfloat16) * (1.0 / HEAD_DIM)
    tile = pl.BlockSpec((1, rows, d), lambda b, s: (b, s, 0))
    out = jax.ShapeDtypeStruct((bsz, seq, d), jnp.bfloat16)
    return pl.pallas_call(
        _qkv_kernel,
        grid=(bsz, seq // rows),
        in_specs=[
            tile,
            pl.BlockSpec((1, N_MOD, d), lambda b, s: (b, 0, 0)),
            _const_spec((1, d)),
            _const_spec((d, 3 * d)),
            _const_spec((POOL_LANES, POOL_LANES)),
            _const_spec((1, d)),
            _const_spec((1, d)),
        ],
        out_specs=[tile, tile, tile],
        out_shape=[out, out, out],
        compiler_params=pltpu.CompilerParams(
            dimension_semantics=("arbitrary", "arbitrary"),
            vmem_limit_bytes=V7X_VMEM_LIMIT_BYTES),
        name="l1_qkv",
    )(x, mod, mix_norm.reshape(1, d), w_qkv.astype(jnp.bfloat16), pool.astype(jnp.bfloat16),
      jnp.tile(q_norm, n_heads).reshape(1, d), jnp.tile(k_norm, n_heads).reshape(1, d))


MASK_BIAS = -1e30
SKIP_LOG2 = 150.0
EXP2_MAX = 126.0
ITEM_PAD = 2


def _attention_kernel(q_ref, k_ref, v_ref, later_ref, bias_ref, o_ref,
                      q_heads, acc, cs, sp16, m32, w16, rs, cmin, items, *, n_blocks):
    blk = ATT_BLOCK
    n_heads = HEADS_PER_LANE_GROUP
    rows = n_heads * blk
    trash = n_blocks

    lane = lax.broadcasted_iota(jnp.int32, (blk, V7X_LANES), 1)
    for i in range(n_blocks):
        q = q_ref[0, i * blk:(i + 1) * blk, :]
        for h in range(n_heads):
            in_head = (lane >= h * HEAD_DIM) & (lane < (h + 1) * HEAD_DIM)
            q_heads[i, h * blk:(h + 1) * blk, :] = jnp.where(in_head, q, jnp.zeros_like(q))
    q_heads[trash] = jnp.zeros((rows, V7X_LANES), q_heads.dtype)
    acc[trash] = jnp.zeros((rows, V7X_LANES), jnp.float32)
    cs[trash] = jnp.zeros((rows, 1), jnp.float32)

    def item(n, dist):
        qi = items[n]
        kj = jnp.minimum(qi, n_blocks - 1) - dist
        return qi, pl.multiple_of(kj * blk, blk)

    def stage1(n, slot, dist, diag):
        qi, k_row = item(n, dist)
        z = lax.dot_general(q_heads[qi], k_ref[0, pl.ds(k_row, blk), :], (((1,), (1,)), ((), ())),
                            preferred_element_type=jnp.float32)
        if diag:
            z = z + bias_ref[...]
        sp = jnp.maximum(jnp.log(1.0 + jnp.exp2(jnp.minimum(z, EXP2_MAX))) * LOG2E, z)
        m32[slot] = z - sp
        sp16[slot] = sp.astype(jnp.bfloat16)
        rs[slot] = jnp.sum(sp, axis=1, keepdims=True)

    def stage2(n, slot, dist, diag):
        qi, _ = item(n, dist)
        t = m32[slot] - _dot(sp16[slot], later_ref[...])
        c = rs[slot]
        if not diag:
            prev = cs[qi]
            t = t - prev
            c = c + prev
        w16[slot] = jnp.exp2(t).astype(jnp.bfloat16)
        cs[qi] = c
        cmin[pl.ds(qi, 1), :] = jnp.broadcast_to(jnp.min(c, axis=0, keepdims=True), (1, V7X_LANES))

    def stage3(n, slot, dist, diag):
        qi, k_row = item(n, dist)
        pv = _dot(w16[slot], v_ref[0, pl.ds(k_row, blk), :])
        if diag:
            acc[qi] = pv
        else:
            acc[qi] += pv

    def run_wave(n_items, kinds):
        def step(it, slot, stages):
            if 1 in stages:
                stage1(it, slot, *kinds[slot])
            if 2 in stages:
                stage2(it - 1, 1 - slot, *kinds[1 - slot])
            if 3 in stages:
                stage3(it - 2, slot, *kinds[slot])

        step(0, 0, (1,))
        step(1, 1, (1, 2))

        def pair(t, carry):
            it = 2 * t + 2
            step(it, 0, (1, 2, 3))
            step(it + 1, 1, (1, 2, 3))
            return carry

        lax.fori_loop(0, (n_items - 2) // 2, pair, 0)
        step(n_items, 0, (2, 3))
        step(n_items + 1, 1, (3,))

    def list_active(dist):
        low = cmin[0:n_blocks, :]
        block = lax.broadcasted_iota(jnp.int32, low.shape, 0)
        active = (low < SKIP_LOG2) & (block >= dist)
        weights = jnp.where(active, jnp.left_shift(1, block), 0).astype(jnp.float32)
        mask = (jnp.sum(weights) * (1.0 / V7X_LANES)).astype(jnp.int32)
        count = jnp.int32(0)
        for i in range(n_blocks):
            items[count] = i
            count = count + ((mask >> i) & 1)
        for p in range(ITEM_PAD):
            items[count + p] = trash
        return count

    for i in range(n_blocks):
        items[2 * i] = i
        items[2 * i + 1] = i if i > 0 else trash
    run_wave(2 * n_blocks, ((0, True), (1, False)))

    def more_waves(state):
        dist, count = state
        run_wave(count + (count & 1), ((dist, False), (dist, False)))
        return dist + 1, list_active(dist + 1)

    lax.while_loop(lambda state: state[1] > 0, more_waves, (jnp.int32(2), list_active(2)))

    for i in range(n_blocks):
        out = acc[i, 0:blk, :]
        for h in range(1, n_heads):
            out = jnp.where(lane >= h * HEAD_DIM, acc[i, h * blk:(h + 1) * blk, :], out)
        o_ref[0, i * blk:(i + 1) * blk, :] = out.astype(o_ref.dtype)


def _attention(q, k, v):
    bsz, seq, d = q.shape
    blk = ATT_BLOCK
    n_heads = HEADS_PER_LANE_GROUP
    assert seq % blk == 0 and d % V7X_LANES == 0
    n_blocks = seq // blk
    rows = n_heads * blk
    idx = jnp.arange(blk)
    later = (idx[:, None] > idx[None, :]).astype(jnp.bfloat16)
    bias = jnp.where(idx[None, :] < idx[:, None], 0.0, MASK_BIAS).astype(jnp.float32)
    bias = jnp.tile(bias, (n_heads, 1))
    seq_spec = pl.BlockSpec((1, seq, V7X_LANES), lambda b, p: (b, 0, p))
    cmin_rows = pl.cdiv(n_blocks + 1, V7X_SUBLANES) * V7X_SUBLANES
    return pl.pallas_call(
        functools.partial(_attention_kernel, n_blocks=n_blocks),
        grid=(bsz, d // V7X_LANES),
        in_specs=[seq_spec, seq_spec, seq_spec, _const_spec((blk, blk)), _const_spec((rows, blk))],
        out_specs=seq_spec,
        out_shape=jax.ShapeDtypeStruct((bsz, seq, d), jnp.bfloat16),
        scratch_shapes=[
            pltpu.VMEM((n_blocks + 1, rows, V7X_LANES), jnp.bfloat16),
            pltpu.VMEM((n_blocks + 1, rows, V7X_LANES), jnp.float32),
            pltpu.VMEM((n_blocks + 1, rows, 1), jnp.float32),
            pltpu.VMEM((2, rows, blk), jnp.bfloat16),
            pltpu.VMEM((2, rows, blk), jnp.float32),
            pltpu.VMEM((2, rows, blk), jnp.bfloat16),
            pltpu.VMEM((2, rows, 1), jnp.float32),
            pltpu.VMEM((cmin_rows, V7X_LANES), jnp.float32),
            pltpu.SMEM((2 * n_blocks + ITEM_PAD,), jnp.int32),
        ],
        compiler_params=pltpu.CompilerParams(
            dimension_semantics=("arbitrary", "arbitrary"),
            vmem_limit_bytes=V7X_VMEM_LIMIT_BYTES),
        name="sb_attention",
    )(q, k, v, later, bias)


def kernel(x, c, l0_ada_w, l0_ada_b, l0_mix_norm, l0_w_in, l0_conv_a_w, l0_conv_b_w, l0_conv_b_b, l0_rg_a_w, l0_rg_a_b, l0_rg_x_w, l0_rg_x_b, l0_rg_lambda, l0_w_out, l0_ffn_norm, l0_ffn_w_gate, l0_ffn_w_up, l0_ffn_w_down, l1_ada_w, l1_ada_b, l1_mix_norm, l1_w_qkv, l1_q_norm, l1_k_norm, l1_w_out, l1_ffn_norm, l1_ffn_w_gate, l1_ffn_w_up, l1_ffn_w_down):
    mod0 = _ada_modulation(c, l0_ada_w, l0_ada_b)
    mod1 = _ada_modulation(c, l1_ada_w, l1_ada_b)

    y = _mixer(x, mod0, l0_mix_norm, l0_w_in, l0_conv_a_w, l0_conv_b_w, l0_conv_b_b,
               l0_rg_a_w, l0_rg_a_b, l0_rg_x_w, l0_rg_x_b, l0_rg_lambda)
    x = _proj_ffn(x, y, mod0, l0_w_out, l0_ffn_norm, l0_ffn_w_gate, l0_ffn_w_up, l0_ffn_w_down)

    q, k, v = _qkv(x, mod1, l1_mix_norm, l1_w_qkv, l1_q_norm, l1_k_norm)
    o = _attention(q, k, v)
    x = _proj_ffn(x, o, mod1, l1_w_out, l1_ffn_norm, l1_ffn_w_gate, l1_ffn_w_up, l1_ffn_w_down)
    return x
```
